```python
import jax, jax.numpy as jnp
from jax import lax
import numpy as np

D_MODEL = 2048
BATCH = 16
SEQ = 2048
DEPTH = 1

CTX_LEN = 256
GRID_W = 64
RET_HEAD_DIM = 256
RET_HEADS = D_MODEL // RET_HEAD_DIM
RET_QK = RET_HEADS * RET_HEAD_DIM
RET_V = RET_HEADS * RET_HEAD_DIM
RET_CHUNK = 128
KEY_SCALE = RET_HEAD_DIM ** -0.5
ROPE_BASE = 10000.0
POOL_WINDOWS = (2, 4, 8, 16)
POOL_WIDTH = D_MODEL
POOL_GROUP = POOL_WIDTH // len(POOL_WINDOWS)
N_BRANCHES = 2
Q0 = 0
K0 = Q0 + RET_QK
V0 = K0 + RET_QK
G0 = V0 + RET_V
P0 = G0 + RET_V
A0 = P0 + POOL_WIDTH
IN_WIDTH = A0 + N_BRANCHES * D_MODEL
N_GROUPS = 8
EXPERTS_PER_GROUP = 8
N_EXPERTS = N_GROUPS * EXPERTS_PER_GROUP
TOP_K = 2
EXPERT_FF = D_MODEL // 2
MOE_BLOCK = 128
EPS = 1e-6

kernel_name = "hybrid_retention_pool_hmoe_dit"


def rmsnorm(x, g):
    x32 = x.astype(jnp.float32)
    y = x32 * lax.rsqrt(jnp.mean(x32 * x32, axis=-1, keepdims=True) + EPS)
    return (y * g.astype(jnp.float32)).astype(x.dtype)


def modulate(h, shift, scale):
    return h * (1.0 + scale) + shift


def split_heads(t):
    b, l, _ = t.shape
    return t.reshape(b, l, RET_HEADS, -1).transpose(0, 2, 1, 3)


def rope_2d(t, rows, cols):
    half = t.shape[-1] // 2
    quarter = half // 2
    freqs = ROPE_BASE ** (-jnp.arange(quarter, dtype=jnp.float32) / quarter)

    def rot(u, pos):
        ang = pos[:, None] * freqs[None, :]
        cos, sin = jnp.cos(ang), jnp.sin(ang)
        u1, u2 = u[..., :quarter], u[..., quarter:]
        return jnp.concatenate([u1 * cos - u2 * sin, u1 * sin + u2 * cos], axis=-1)

    return jnp.concatenate([rot(t[..., :half], rows), rot(t[..., half:], cols)], axis=-1)


def retention_chunkwise(q, k, v, log_gamma, s0):
    b, h, l, dk = q.shape
    dv = v.shape[-1]
    n_chunks = l // RET_CHUNK
    idx = jnp.arange(RET_CHUNK, dtype=jnp.float32)
    diff = idx[:, None] - idx[None, :]
    decay_inner = jnp.where(diff >= 0, jnp.exp(log_gamma[:, None, None] * jnp.maximum(diff, 0.0)), 0.0)
    decay_q = jnp.exp(log_gamma[:, None] * (idx + 1.0))[None, :, :, None]
    decay_k = jnp.exp(log_gamma[:, None] * (RET_CHUNK - 1.0 - idx))[None, :, :, None]
    decay_chunk = jnp.exp(log_gamma * RET_CHUNK)[None, :, None, None]

    def to_chunks(t):
        return jnp.moveaxis(t.reshape(b, h, n_chunks, RET_CHUNK, t.shape[-1]), 2, 0)

    def step(state, inp):
        qc, kc, vc = inp
        scores = jnp.einsum('bhid,bhjd->bhij', qc, kc) * decay_inner[None]
        o = jnp.einsum('bhij,bhje->bhie', scores, vc) + jnp.einsum('bhid,bhde->bhie', qc, state) * decay_q
        state = state * decay_chunk + jnp.einsum('bhjd,bhje->bhde', kc * decay_k, vc)
        return state, o

    _, o = lax.scan(step, s0, (to_chunks(q), to_chunks(k), to_chunks(v)))
    return jnp.moveaxis(o, 0, 2).reshape(b, h, l, dv)


def retention_final_state(k, v, log_gamma):
    l = k.shape[2]
    w = jnp.exp(log_gamma[:, None] * (l - 1.0 - jnp.arange(l, dtype=jnp.float32)))
    return jnp.einsum('bhld,bhle->bhde', k * w[None, :, :, None], v)


def retention_branch(q, k, v, g, log_gammas, s_f, s_b):
    fwd = retention_chunkwise(q, k, v, log_gammas[0], s_f)
    bwd = jnp.flip(retention_chunkwise(jnp.flip(q, 2), jnp.flip(k, 2), jnp.flip(v, 2), log_gammas[1], s_b), 2)
    o = fwd + bwd
    o = o * lax.rsqrt(jnp.mean(o * o, axis=-1, keepdims=True) + EPS)
    b, h, l, dv = o.shape
    o = o.transpose(0, 2, 1, 3).reshape(b, l, h * dv)
    return jax.nn.silu(g.astype(jnp.float32)) * o


def box_mean_1d(t, w, axis):
    l = t.shape[axis]
    lo = w // 2
    hi = w - lo - 1
    pad = [(0, 0)] * t.ndim
    pad[axis] = (1, 0)
    cs = jnp.pad(jnp.cumsum(t, axis=axis), pad)
    pos = jnp.arange(l)
    start = jnp.clip(pos - lo, 0, l)
    end = jnp.clip(pos + hi + 1, 0, l)
    s = jnp.take(cs, end, axis=axis) - jnp.take(cs, start, axis=axis)
    cshape = [1] * t.ndim
    cshape[axis] = l
    return s / (end - start).astype(jnp.float32).reshape(cshape)


def pool_mixer(u, group_w, pool_scale, on_grid):
    b, l, _ = u.shape
    outs = []
    for gi, w in enumerate(POOL_WINDOWS):
        ug = u[..., gi * POOL_GROUP:(gi + 1) * POOL_GROUP].astype(jnp.float32)
        if on_grid:
            rows = l // GRID_W
            ug2 = ug.reshape(b, rows, GRID_W, POOL_GROUP)
            m = box_mean_1d(box_mean_1d(ug2, w, 2), w, 1).reshape(b, l, POOL_GROUP)
        else:
            m = box_mean_1d(ug, w, 1)
        outs.append(jnp.einsum('blc,cd->bld', (m - ug).astype(u.dtype), group_w[gi]))
    return jnp.concatenate(outs, axis=-1) * pool_scale


def token_mixer(h, w_in, log_gammas, s_f, s_b, pool_w, pool_scale, w_ret_out, w_pool_out, w_o, on_grid):
    proj = h @ w_in
    q = split_heads(proj[..., Q0:K0]).astype(jnp.float32)
    k = split_heads(proj[..., K0:V0]).astype(jnp.float32) * KEY_SCALE
    v = split_heads(proj[..., V0:G0]).astype(jnp.float32)
    if on_grid:
        pos = jnp.arange(h.shape[1])
        rows = (pos // GRID_W).astype(jnp.float32)
        cols = (pos % GRID_W).astype(jnp.float32)
        q = rope_2d(q, rows, cols)
        k = rope_2d(k, rows, cols)
    y_ret = retention_branch(q, k, v, proj[..., G0:P0], log_gammas, s_f, s_b).astype(h.dtype)
    y_pool = pool_mixer(proj[..., P0:A0], pool_w, pool_scale, on_grid)
    gate_ret = jax.nn.sigmoid(proj[..., A0:A0 + D_MODEL])
    gate_pool = jax.nn.sigmoid(proj[..., A0 + D_MODEL:A0 + 2 * D_MODEL])
    merged = gate_ret * (y_ret @ w_ret_out) + gate_pool * (y_pool @ w_pool_out)
    return merged @ w_o


def context_states(h_ctx, w_in, log_gammas):
    k = split_heads(h_ctx @ w_in[:, K0:V0]).astype(jnp.float32) * KEY_SCALE
    v = split_heads(h_ctx @ w_in[:, V0:G0]).astype(jnp.float32)
    s_f = retention_final_state(k, v, log_gammas[0])
    s_b = retention_final_state(jnp.flip(k, 2), jnp.flip(v, 2), log_gammas[1])
    return s_f, s_b


def hierarchical_moe(h, w_rg, b_rg, w_re, b_re, w_gate, w_up, w_down):
    b, l, d = h.shape
    t = b * l
    hf = h.reshape(t, d)
    p_group = jax.nn.softmax((hf @ w_rg).astype(jnp.float32) + b_rg.astype(jnp.float32), axis=-1)
    pg_top, g_top = lax.top_k(p_group, 1)
    logits_e = ((hf @ w_re).astype(jnp.float32) + b_re.astype(jnp.float32)).reshape(t, N_GROUPS, EXPERTS_PER_GROUP)
    logits_sel = jnp.take_along_axis(logits_e, g_top[:, :, None], axis=1)[:, 0]
    pe_top, e_local = lax.top_k(jax.nn.softmax(logits_sel, axis=-1), TOP_K)
    weights = pg_top * pe_top
    expert_ids = g_top * EXPERTS_PER_GROUP + e_local

    n_assign = t * TOP_K
    e_flat = expert_ids.reshape(-1)
    tok_flat = jnp.repeat(jnp.arange(t, dtype=jnp.int32), TOP_K)
    order = jnp.argsort(e_flat)
    se = e_flat[order]
    stok = tok_flat[order]
    swt = weights.reshape(-1)[order]
    counts = jnp.bincount(e_flat, length=N_EXPERTS)
    padded = ((counts + MOE_BLOCK - 1) // MOE_BLOCK) * MOE_BLOCK
    start = jnp.cumsum(counts) - counts
    pend = jnp.cumsum(padded)
    pstart = pend - padded
    dest = pstart[se] + jnp.arange(n_assign) - start[se]
    n_blocks = -(-n_assign // MOE_BLOCK) + N_EXPERTS
    n_rows = n_blocks * MOE_BLOCK
    row_tok = jnp.full((n_rows,), t, dtype=jnp.int32).at[dest].set(stok)
    block_e = jnp.clip(jnp.searchsorted(pend, jnp.arange(n_blocks) * MOE_BLOCK, side='right'), 0, N_EXPERTS - 1)
    xin = jnp.concatenate([hf, jnp.zeros((1, d), hf.dtype)], axis=0)[row_tok].reshape(n_blocks, MOE_BLOCK, d)

    def expert_block(args):
        xb, e = args
        return (jax.nn.silu(xb @ w_gate[e]) * (xb @ w_up[e])) @ w_down[e]

    yb = lax.map(expert_block, (xin, block_e)).reshape(n_rows, d)
    y = jnp.zeros((t, d), yb.dtype).at[stok].add(yb[dest] * swt[:, None].astype(yb.dtype))
    return y.reshape(b, l, d)


def setup_inputs(seed: int = 0) -> dict:
    key = jax.random.key(seed)
    ks = jax.random.split(key, 24)
    f32 = jnp.float32

    def nrm(k, shape, scale):
        return jax.random.normal(k, shape, f32) * scale

    base_decay = jnp.log(-jnp.log1p(-(2.0 ** (-5.0 - jnp.arange(RET_HEADS, dtype=f32)))))
    return {
        "x": nrm(ks[0], (BATCH, SEQ, D_MODEL), 1.0),
        "c": nrm(ks[1], (BATCH, D_MODEL), 1.0),
        "ctx": nrm(ks[2], (BATCH, CTX_LEN, D_MODEL), 1.0),
        "c_ctx": nrm(ks[3], (D_MODEL,), 1.0),
        "w_mod": nrm(ks[4], (DEPTH, D_MODEL, 6 * D_MODEL), 0.5 * D_MODEL ** -0.5),
        "b_mod": nrm(ks[5], (DEPTH, 6 * D_MODEL), 0.02),
        "g_pre_mix": 1.0 + nrm(ks[6], (DEPTH, D_MODEL), 0.05),
        "g_post_mix": 1.0 + nrm(ks[7], (DEPTH, D_MODEL), 0.05),
        "g_pre_ffn": 1.0 + nrm(ks[8], (DEPTH, D_MODEL), 0.05),
        "g_post_ffn": 1.0 + nrm(ks[9], (DEPTH, D_MODEL), 0.05),
        "w_in": nrm(ks[10], (DEPTH, D_MODEL, IN_WIDTH), D_MODEL ** -0.5),
        "ret_decay": base_decay[None, None, :] + nrm(ks[11], (DEPTH, 2, RET_HEADS), 0.01),
        "pool_w": nrm(ks[12], (DEPTH, len(POOL_WINDOWS), POOL_GROUP, POOL_GROUP), POOL_GROUP ** -0.5),
        "pool_scale": 1.0 + nrm(ks[13], (DEPTH, POOL_WIDTH), 0.1),
        "w_ret_out": nrm(ks[14], (DEPTH, RET_V, D_MODEL), RET_V ** -0.5),
        "w_pool_out": nrm(ks[15], (DEPTH, POOL_WIDTH, D_MODEL), POOL_WIDTH ** -0.5),
        "w_o": nrm(ks[16], (DEPTH, D_MODEL, D_MODEL), D_MODEL ** -0.5),
        "w_router_group": nrm(ks[17], (DEPTH, D_MODEL, N_GROUPS), D_MODEL ** -0.5),
        "b_router_group": nrm(ks[18], (DEPTH, N_GROUPS), 0.01),
        "w_router_expert": nrm(ks[19], (DEPTH, D_MODEL, N_EXPERTS), D_MODEL ** -0.5),
        "b_router_expert": nrm(ks[20], (DEPTH, N_EXPERTS), 0.01),
        "w_exp_gate": nrm(ks[21], (DEPTH, N_EXPERTS, D_MODEL, EXPERT_FF), D_MODEL ** -0.5),
        "w_exp_up": nrm(ks[22], (DEPTH, N_EXPERTS, D_MODEL, EXPERT_FF), D_MODEL ** -0.5),
        "w_exp_down": nrm(ks[23], (DEPTH, N_EXPERTS, EXPERT_FF, D_MODEL), EXPERT_FF ** -0.5),
    }


def reference(x, c, ctx, c_ctx, w_mod, b_mod, g_pre_mix, g_post_mix, g_pre_ffn, g_post_ffn, w_in,
              ret_decay, pool_w, pool_scale, w_ret_out, w_pool_out, w_o, w_router_group, b_router_group,
              w_router_expert, b_router_expert, w_exp_gate, w_exp_up, w_exp_down):
    for i in range(DEPTH):
        last = i == DEPTH - 1
        mod = jax.nn.silu(c) @ w_mod[i] + b_mod[i]
        sh_m, sc_m, gt_m, sh_f, sc_f, gt_f = [m[:, None, :] for m in jnp.split(mod, 6, axis=-1)]
        mod_c = jax.nn.silu(c_ctx) @ w_mod[i] + b_mod[i]
        csh_m, csc_m, cgt_m, csh_f, csc_f, cgt_f = jnp.split(mod_c, 6, axis=-1)
        log_gammas = -jnp.exp(ret_decay[i].astype(jnp.float32))

        h_ctx = modulate(rmsnorm(ctx, g_pre_mix[i]), csh_m, csc_m)
        s_f, s_b = context_states(h_ctx, w_in[i], log_gammas)

        h = modulate(rmsnorm(x, g_pre_mix[i]), sh_m, sc_m)
        mix = token_mixer(h, w_in[i], log_gammas, s_f, s_b, pool_w[i], pool_scale[i],
                          w_ret_out[i], w_pool_out[i], w_o[i], True)
        x = x + gt_m * rmsnorm(mix, g_post_mix[i])

        h2 = modulate(rmsnorm(x, g_pre_ffn[i]), sh_f, sc_f)
        ffn = hierarchical_moe(h2, w_router_group[i], b_router_group[i], w_router_expert[i], b_router_expert[i],
                               w_exp_gate[i], w_exp_up[i], w_exp_down[i])
        x = x + gt_f * rmsnorm(ffn, g_post_ffn[i])

        if not last:
            zero_state = jnp.zeros(s_f.shape, jnp.float32)
            mix_c = token_mixer(h_ctx, w_in[i], log_gammas, zero_state, zero_state, pool_w[i], pool_scale[i],
                                w_ret_out[i], w_pool_out[i], w_o[i], False)
            ctx = ctx + cgt_m * rmsnorm(mix_c, g_post_mix[i])
            h2c = modulate(rmsnorm(ctx, g_pre_ffn[i]), csh_f, csc_f)
            ffn_c = hierarchical_moe(h2c, w_router_group[i], b_router_group[i], w_router_expert[i],
                                     b_router_expert[i], w_exp_gate[i], w_exp_up[i], w_exp_down[i])
            ctx = ctx + cgt_f * rmsnorm(ffn_c, g_post_ffn[i])
    return x
```

```python
import functools

import jax
import jax.numpy as jnp
import numpy as np
from jax import lax
from jax.experimental import pallas as pl
from jax.experimental.pallas import tpu as pltpu

F32 = jnp.float32
BF16 = jnp.bfloat16

HEAD_DIM = 256
GRID_W = 64
ROPE_BASE = 10000.0
POOL_WINDOWS = (2, 4, 8, 16)
N_GROUPS = 8
EXPERTS_PER_GROUP = 8
N_EXPERTS = N_GROUPS * EXPERTS_PER_GROUP
TOP_K = 2
EPS = 1e-6
KEY_SCALE = HEAD_DIM ** -0.5

RET_CHUNK = 256
MOE_ROWS = 256
ROUTER_LANES = 128
V7X_VMEM_LIMIT = 56 * 1024 * 1024


def _cparams(sem, vmem=V7X_VMEM_LIMIT):
    return pltpu.CompilerParams(dimension_semantics=sem, vmem_limit_bytes=vmem)


def _sigmoid(x):
    return 1.0 / (1.0 + jnp.exp(-x))


def _silu(x):
    return x * _sigmoid(x)


def _rms(x, g):
    return x * lax.rsqrt(jnp.mean(x * x, axis=-1, keepdims=True) + EPS) * g


def _mod_kernel(c_ref, w_ref, b_ref, o_ref):
    s = _silu(c_ref[...]).astype(BF16)
    o_ref[...] = jnp.dot(s, w_ref[...].astype(BF16), preferred_element_type=F32) + b_ref[...]


def _mod(cc, w_mod, b_mod):
    rows, d = cc.shape
    n = w_mod.shape[1]
    tn = min(1024, n)
    return pl.pallas_call(
        _mod_kernel,
        grid=(n // tn,),
        in_specs=[pl.BlockSpec((rows, d), lambda j: (0, 0)),
                  pl.BlockSpec((d, tn), lambda j: (0, j)),
                  pl.BlockSpec((1, tn), lambda j: (0, j))],
        out_specs=pl.BlockSpec((rows, tn), lambda j: (0, j)),
        out_shape=jax.ShapeDtypeStruct((rows, n), F32),
        compiler_params=_cparams(("parallel",)),
        name="mod",
    )(cc, w_mod, b_mod)


def _inproj_kernel(x_ref, g_ref, sh_ref, sc_ref, w_ref, o_ref, h_scr):
    @pl.when(pl.program_id(1) == 0)
    def _():
        y = _rms(x_ref[...], g_ref[...])
        h_scr[...] = (y * (1.0 + sc_ref[0]) + sh_ref[0]).astype(BF16)

    o_ref[...] = jnp.dot(h_scr[...], w_ref[...], preferred_element_type=F32).astype(o_ref.dtype)


def _inproj(x2, g, sh, sc, w, rows_per_mod, col_tile0, n_col_tiles, tn):
    m, d = x2.shape
    tm = min(1024, rows_per_mod, m)
    mod_of = (lambda i: (i * tm) // rows_per_mod) if sh.shape[0] > 1 else (lambda i: 0)
    return pl.pallas_call(
        _inproj_kernel,
        grid=(m // tm, n_col_tiles),
        in_specs=[pl.BlockSpec((tm, d), lambda i, j: (i, 0)),
                  pl.BlockSpec((1, d), lambda i, j: (0, 0)),
                  pl.BlockSpec((1, 1, d), lambda i, j: (mod_of(i), 0, 0)),
                  pl.BlockSpec((1, 1, d), lambda i, j: (mod_of(i), 0, 0)),
                  pl.BlockSpec((d, tn), lambda i, j: (0, j + col_tile0))],
        out_specs=pl.BlockSpec((tm, tn), lambda i, j: (i, j)),
        out_shape=jax.ShapeDtypeStruct((m, n_col_tiles * tn), BF16),
        scratch_shapes=[pltpu.VMEM((tm, d), BF16)],
        compiler_params=_cparams(("parallel", "arbitrary")),
        name="inproj",
    )(x2, g, sh, sc, w)


def _ret_kernel(lg_ref, q_ref, k_ref, v_ref, g_ref, kc_ref, vc_ref, cos_ref, sin_ref, o_ref,
                qs, ks, os_, sf, sb, dmat, dqf, dkf, dqb, dkb):
    h = pl.program_id(1)
    lgf = lg_ref[0, h]
    lgb = lg_ref[1, h]
    seq = q_ref.shape[0]
    lc = kc_ref.shape[0]
    c = RET_CHUNK
    n_chunks = seq // c
    half = HEAD_DIM // 2

    def rope(t_ref):
        t = t_ref[...].astype(F32)
        sw = jnp.concatenate([pltpu.roll(t[:, :half], half // 2, 1),
                              pltpu.roll(t[:, half:], half // 2, 1)], axis=1)
        return t * cos_ref[...] + sw * sin_ref[...]

    qs[...] = rope(q_ref).astype(BF16)
    ks[...] = (rope(k_ref) * KEY_SCALE).astype(BF16)

    ri = lax.broadcasted_iota(jnp.int32, (c, c), 0)
    ci = lax.broadcasted_iota(jnp.int32, (c, c), 1)
    diff = (ri - ci).astype(F32)
    dmat[...] = (jnp.where(diff >= 0, jnp.exp(lgf * jnp.maximum(diff, 0.0)), 0.0)
                 + jnp.where(diff <= 0, jnp.exp(lgb * jnp.maximum(-diff, 0.0)), 0.0))
    rows = lax.broadcasted_iota(jnp.int32, (c, HEAD_DIM), 0).astype(F32)
    dqf[...] = jnp.exp(lgf * (rows + 1.0))
    dkf[...] = jnp.exp(lgf * (c - 1.0 - rows))
    dqb[...] = jnp.exp(lgb * (c - rows))
    dkb[...] = jnp.exp(lgb * rows)
    gf_c = dqf[c - 1:c, :]
    gb_c = dqb[0:1, :]

    pos = lax.broadcasted_iota(jnp.int32, (lc, HEAD_DIM), 0).astype(F32)
    kc = kc_ref[...].astype(F32) * KEY_SCALE
    vc = vc_ref[...]
    tdot = functools.partial(lax.dot_general, dimension_numbers=(((0,), (0,)), ((), ())),
                             preferred_element_type=F32)
    sf[...] = tdot((kc * jnp.exp(lgf * (lc - 1.0 - pos))).astype(BF16), vc)
    sb[...] = tdot((kc * jnp.exp(lgb * pos)).astype(BF16), vc)

    def fwd(i, carry):
        r0 = pl.multiple_of(i * c, c)
        qc = qs[pl.ds(r0, c), :]
        kk = ks[pl.ds(r0, c), :]
        vv = v_ref[pl.ds(r0, c), :]
        s = lax.dot_general(qc, kk, (((1,), (1,)), ((), ())), preferred_element_type=F32)
        p = (s * dmat[...]).astype(BF16)
        st = sf[...]
        o = jnp.dot(p, vv, preferred_element_type=F32)
        o = o + jnp.dot(qc, st.astype(BF16), preferred_element_type=F32) * dqf[...]
        os_[pl.ds(r0, c), :] = o
        kd = (kk.astype(F32) * dkf[...]).astype(BF16)
        sf[...] = st * gf_c + tdot(kd, vv)
        return carry

    lax.fori_loop(0, n_chunks, fwd, 0)

    def bwd(j, carry):
        i = n_chunks - 1 - j
        r0 = pl.multiple_of(i * c, c)
        qc = qs[pl.ds(r0, c), :]
        kk = ks[pl.ds(r0, c), :]
        vv = v_ref[pl.ds(r0, c), :]
        st = sb[...]
        o = jnp.dot(qc, st.astype(BF16), preferred_element_type=F32) * dqb[...]
        os_[pl.ds(r0, c), :] = os_[pl.ds(r0, c), :] + o
        kd = (kk.astype(F32) * dkb[...]).astype(BF16)
        sb[...] = st * gb_c + tdot(kd, vv)
        return carry

    lax.fori_loop(0, n_chunks, bwd, 0)

    o = os_[...]
    o = o * lax.rsqrt(jnp.mean(o * o, axis=-1, keepdims=True) + EPS)
    o_ref[...] = (_silu(g_ref[...].astype(F32)) * o).astype(o_ref.dtype)


def _retention(proj3, kvc3, log_gammas, cos_t, sin_t, heads):
    b, seq, _ = proj3.shape
    lc = kvc3.shape[1]
    hd = HEAD_DIM
    blk = lambda off: pl.BlockSpec((None, seq, hd), lambda bi, hi: (bi, 0, off + hi))
    cblk = lambda off: pl.BlockSpec((None, lc, hd), lambda bi, hi: (bi, 0, off + hi))
    tab = pl.BlockSpec((seq, hd), lambda bi, hi: (0, 0))
    c = RET_CHUNK
    return pl.pallas_call(
        _ret_kernel,
        grid=(b, heads),
        in_specs=[pl.BlockSpec(memory_space=pltpu.SMEM),
                  blk(0), blk(heads), blk(2 * heads), blk(3 * heads),
                  cblk(0), cblk(heads), tab, tab],
        out_specs=pl.BlockSpec((None, seq, hd), lambda bi, hi: (bi, 0, hi)),
        out_shape=jax.ShapeDtypeStruct((b, seq, heads * hd), BF16),
        scratch_shapes=[pltpu.VMEM((seq, hd), BF16), pltpu.VMEM((seq, hd), BF16),
                        pltpu.VMEM((seq, hd), F32),
                        pltpu.VMEM((hd, hd), F32), pltpu.VMEM((hd, hd), F32),
                        pltpu.VMEM((c, c), F32),
                        pltpu.VMEM((c, hd), F32), pltpu.VMEM((c, hd), F32),
                        pltpu.VMEM((c, hd), F32), pltpu.VMEM((c, hd), F32)],
        compiler_params=_cparams(("parallel", "parallel")),
        name="retention",
    )(log_gammas, proj3, proj3, proj3, proj3, kvc3, kvc3, cos_t, sin_t)


def _pool_kernel(u_ref, bc_ref, cnt_ref, gw_ref, ps_ref, o_ref):
    seq = u_ref.shape[0]
    pg = gw_ref.shape[1]
    blk = bc_ref.shape[1]
    n_slab = seq // GRID_W
    for gi, w in enumerate(POOL_WINDOWS):
        lo = w // 2
        hi = w - lo - 1
        ub = u_ref[:, gi * pg:(gi + 1) * pg]
        s1 = jnp.concatenate(
            [jnp.dot(bc_ref[gi], ub[r * blk:(r + 1) * blk], preferred_element_type=F32)
             for r in range(seq // blk)], axis=0)
        zpad = lambda n: jnp.zeros((n * GRID_W, pg), F32)
        a = jnp.concatenate([zpad(lo), s1, zpad(hi)], axis=0) if hi else jnp.concatenate([zpad(lo), s1], axis=0)
        step = 1
        while step < w:
            n_rows = a.shape[0] - step * GRID_W
            a = a[:n_rows] + a[step * GRID_W:]
            step *= 2
        assert a.shape[0] == n_slab * GRID_W
        cnt = cnt_ref[gi]
        m = a / (cnt if pg == 128 else jnp.concatenate([cnt] * (pg // 128), axis=1))
        d = (m - ub.astype(F32)).astype(BF16)
        y = jnp.dot(d, gw_ref[gi], preferred_element_type=F32) * ps_ref[:, gi * pg:(gi + 1) * pg]
        o_ref[:, gi * pg:(gi + 1) * pg] = y.astype(o_ref.dtype)


def _pool_tables(seq, blk):
    rows = seq // GRID_W
    bands = np.zeros((len(POOL_WINDOWS), blk, blk), np.float32)
    cnts = np.zeros((len(POOL_WINDOWS), seq, 128), np.float32)
    t = np.arange(seq)
    r, c = t // GRID_W, t % GRID_W
    for gi, w in enumerate(POOL_WINDOWS):
        lo = w // 2
        hi = w - lo - 1
        col = np.arange(GRID_W)
        band = ((col[None, :] >= col[:, None] - lo) & (col[None, :] <= col[:, None] + hi)).astype(np.float32)
        bands[gi] = np.kron(np.eye(blk // GRID_W, dtype=np.float32), band)
        cc = np.minimum(c + hi + 1, GRID_W) - np.maximum(c - lo, 0)
        cr = np.minimum(r + hi + 1, rows) - np.maximum(r - lo, 0)
        cnts[gi] = (cc * cr).astype(np.float32)[:, None]
    return jnp.asarray(bands, BF16), jnp.asarray(cnts, F32)


def _pool(proj3, col_block, pool_w, pool_scale):
    b, seq, _ = proj3.shape
    n_g, pg, _ = pool_w.shape
    pw = n_g * pg
    blk = min(256, seq)
    bands, cnts = _pool_tables(seq, blk)
    return pl.pallas_call(
        _pool_kernel,
        grid=(b,),
        in_specs=[pl.BlockSpec((None, seq, pw), lambda bi: (bi, 0, col_block)),
                  pl.BlockSpec((n_g, blk, blk), lambda bi: (0, 0, 0)),
                  pl.BlockSpec((n_g, seq, 128), lambda bi: (0, 0, 0)),
                  pl.BlockSpec((n_g, pg, pg), lambda bi: (0, 0, 0)),
                  pl.BlockSpec((1, pw), lambda bi: (0, 0))],
        out_specs=pl.BlockSpec((None, seq, pw), lambda bi: (bi, 0, 0)),
        out_shape=jax.ShapeDtypeStruct((b, seq, pw), BF16),
        compiler_params=_cparams(("parallel",)),
        name="pool",
    )(proj3, bands, cnts, pool_w, pool_scale)


def _merge_kernel(yr_ref, yp_ref, gr_ref, gp_ref, wr_ref, wp_ref, o_ref):
    a = jnp.dot(yr_ref[...], wr_ref[...], preferred_element_type=F32)
    p = jnp.dot(yp_ref[...], wp_ref[...], preferred_element_type=F32)
    o_ref[...] = (_sigmoid(gr_ref[...].astype(F32)) * a + _sigmoid(gp_ref[...].astype(F32)) * p).astype(o_ref.dtype)


def _merge(y_ret, y_pool, proj, gate_col0, w_ret_out, w_pool_out):
    t, d = y_ret.shape
    tm = min(512, t)
    tn = min(1024, d)
    nj = d // tn
    return pl.pallas_call(
        _merge_kernel,
        grid=(t // tm, nj),
        in_specs=[pl.BlockSpec((tm, d), lambda i, j: (i, 0)),
                  pl.BlockSpec((tm, d), lambda i, j: (i, 0)),
                  pl.BlockSpec((tm, tn), lambda i, j: (i, gate_col0 * nj + j)),
                  pl.BlockSpec((tm, tn), lambda i, j: (i, (gate_col0 + 1) * nj + j)),
                  pl.BlockSpec((d, tn), lambda i, j: (0, j)),
                  pl.BlockSpec((d, tn), lambda i, j: (0, j))],
        out_specs=pl.BlockSpec((tm, tn), lambda i, j: (i, j)),
        out_shape=jax.ShapeDtypeStruct((t, d), BF16),
        compiler_params=_cparams(("parallel", "parallel")),
        name="merge",
    )(y_ret, y_pool, proj, proj, w_ret_out, w_pool_out)


def _post_kernel(m_ref, x_ref, wo_ref, gpm_ref, gpf_ref, gt_ref, sh_ref, sc_ref, wr_ref, br_ref,
                 x1_ref, h2_ref, lg_ref):
    mix = jnp.dot(m_ref[...], wo_ref[...], preferred_element_type=F32)
    x1 = x_ref[...] + gt_ref[0] * _rms(mix, gpm_ref[...])
    x1_ref[...] = x1
    h2 = _rms(x1, gpf_ref[...]) * (1.0 + sc_ref[0]) + sh_ref[0]
    h2_ref[...] = h2
    lg_ref[...] = jnp.dot(h2.astype(BF16), wr_ref[...], preferred_element_type=F32) + br_ref[...]


def _post(merged, x2, w_o, g_post_mix, g_pre_ffn, gt_m, sh_f, sc_f, w_router, b_router, seq):
    t, d = x2.shape
    tm = min(256, seq)
    per_b = seq // tm
    vec = pl.BlockSpec((1, d), lambda i: (0, 0))
    bvec = pl.BlockSpec((1, 1, d), lambda i: (i // per_b, 0, 0))
    row = pl.BlockSpec((tm, d), lambda i: (i, 0))
    return pl.pallas_call(
        _post_kernel,
        grid=(t // tm,),
        in_specs=[row, row,
                  pl.BlockSpec((d, d), lambda i: (0, 0)),
                  vec, vec, bvec, bvec, bvec,
                  pl.BlockSpec((d, ROUTER_LANES), lambda i: (0, 0)),
                  pl.BlockSpec((1, ROUTER_LANES), lambda i: (0, 0))],
        out_specs=[row, row, pl.BlockSpec((tm, ROUTER_LANES), lambda i: (i, 0))],
        out_shape=[jax.ShapeDtypeStruct((t, d), F32), jax.ShapeDtypeStruct((t, d), F32),
                   jax.ShapeDtypeStruct((t, ROUTER_LANES), F32)],
        compiler_params=_cparams(("parallel",)),
        name="post",
    )(merged, x2, w_o, g_post_mix, g_pre_ffn, gt_m, sh_f, sc_f, w_router, b_router)


def _gather_kernel(idx_ref, src_ref, dst_ref, sems):
    i = pl.program_id(0)
    n = pl.num_programs(0)
    ch = idx_ref.shape[1]
    slot = i % 2

    def issue(r, carry):
        tok = idx_ref[i, r]
        pltpu.make_async_copy(src_ref.at[pl.ds(tok, 1)], dst_ref.at[pl.ds(i * ch + r, 1)],
                              sems.at[slot]).start()
        return carry

    lax.fori_loop(0, ch, issue, 0)

    def wait_chunk(s):
        pltpu.make_async_copy(src_ref.at[pl.ds(0, ch)], dst_ref.at[pl.ds(0, ch)], sems.at[s]).wait()

    @pl.when(i > 0)
    def _():
        wait_chunk(1 - slot)

    @pl.when(i == n - 1)
    def _():
        wait_chunk(slot)


def _gather_rows(src, idx2):
    n_chunks, ch = idx2.shape
    d = src.shape[1]
    return pl.pallas_call(
        _gather_kernel,
        grid_spec=pltpu.PrefetchScalarGridSpec(
            num_scalar_prefetch=1,
            grid=(n_chunks,),
            in_specs=[pl.BlockSpec(memory_space=pl.ANY)],
            out_specs=pl.BlockSpec(memory_space=pl.ANY),
            scratch_shapes=[pltpu.SemaphoreType.DMA((2,))]),
        out_shape=jax.ShapeDtypeStruct((n_chunks * ch, d), src.dtype),
        compiler_params=_cparams(("arbitrary",)),
        name="gather",
    )(idx2, src)


def _expert_kernel(be_ref, nu_ref, x_ref, wg_ref, wu_ref, wd_ref, o_ref):
    i = pl.program_id(0)

    @pl.when(i < nu_ref[0])
    def _():
        x = x_ref[...].astype(BF16)
        g = jnp.dot(x, wg_ref[...], preferred_element_type=F32)
        u = jnp.dot(x, wu_ref[...], preferred_element_type=F32)
        a = (_silu(g) * u).astype(BF16)
        o_ref[...] = jnp.dot(a, wd_ref[...], preferred_element_type=F32).astype(o_ref.dtype)

    @pl.when(i >= nu_ref[0])
    def _():
        o_ref[...] = jnp.zeros(o_ref.shape, o_ref.dtype)


def _experts(xin, block_e, n_used, w_gate, w_up, w_down):
    n_rows, d = xin.shape
    ff = w_gate.shape[2]
    bm = MOE_ROWS
    return pl.pallas_call(
        _expert_kernel,
        grid_spec=pltpu.PrefetchScalarGridSpec(
            num_scalar_prefetch=2,
            grid=(n_rows // bm,),
            in_specs=[pl.BlockSpec((bm, d), lambda i, be, nu: (i, 0)),
                      pl.BlockSpec((None, d, ff), lambda i, be, nu: (be[i], 0, 0)),
                      pl.BlockSpec((None, d, ff), lambda i, be, nu: (be[i], 0, 0)),
                      pl.BlockSpec((None, ff, d), lambda i, be, nu: (be[i], 0, 0))],
            out_specs=pl.BlockSpec((bm, d), lambda i, be, nu: (i, 0))),
        out_shape=jax.ShapeDtypeStruct((n_rows, d), F32),
        compiler_params=_cparams(("arbitrary",)),
        name="experts",
    )(block_e, n_used, xin, w_gate, w_up, w_down)


def _combine_kernel(pos_ref, wt_ref, x1_ref, g_ref, gt_ref, yb_ref, o_ref, buf, sems):
    i = pl.program_id(0)
    n = pl.num_programs(0)
    tm = x1_ref.shape[0]
    slot = i % 2

    def issue(step, s):
        def body(r, carry):
            for kk in range(TOP_K):
                row = pos_ref[step, TOP_K * r + kk]
                pltpu.make_async_copy(yb_ref.at[pl.ds(row, 1)], buf.at[s, kk, pl.ds(r, 1)], sems.at[s]).start()
            return carry
        lax.fori_loop(0, tm, body, 0)

    @pl.when(i == 0)
    def _():
        issue(0, 0)

    @pl.when(i + 1 < n)
    def _():
        issue(i + 1, 1 - slot)

    for kk in range(TOP_K):
        pltpu.make_async_copy(yb_ref.at[pl.ds(0, tm)], buf.at[slot, kk], sems.at[slot]).wait()

    wt = wt_ref[...]
    ffn = buf[slot, 0] * wt[:, 0:1] + buf[slot, 1] * wt[:, 1:2]
    o_ref[...] = x1_ref[...] + gt_ref[0] * _rms(ffn, g_ref[...])


def _combine(yb, pos2, wts, x1, g_post_ffn, gt_f, seq):
    t, d = x1.shape
    tm = pos2.shape[1] // TOP_K
    per_b = seq // tm
    return pl.pallas_call(
        _combine_kernel,
        grid_spec=pltpu.PrefetchScalarGridSpec(
            num_scalar_prefetch=1,
            grid=(t // tm,),
            in_specs=[pl.BlockSpec((tm, TOP_K), lambda i, p: (i, 0)),
                      pl.BlockSpec((tm, d), lambda i, p: (i, 0)),
                      pl.BlockSpec((1, d), lambda i, p: (0, 0)),
                      pl.BlockSpec((1, 1, d), lambda i, p: (i // per_b, 0, 0)),
                      pl.BlockSpec(memory_space=pl.ANY)],
            out_specs=pl.BlockSpec((tm, d), lambda i, p: (i, 0)),
            scratch_shapes=[pltpu.VMEM((2, TOP_K, tm, d), F32), pltpu.SemaphoreType.DMA((2,))]),
        out_shape=jax.ShapeDtypeStruct((t, d), F32),
        compiler_params=_cparams(("arbitrary",)),
        name="combine",
    )(pos2, wts, x1, g_post_ffn, gt_f, yb)


def _rope_tables(seq):
    quarter = HEAD_DIM // 4
    freqs = ROPE_BASE ** (-jnp.arange(quarter, dtype=F32) / quarter)
    pos = jnp.arange(seq)
    rows = (pos // GRID_W).astype(F32)
    cols = (pos % GRID_W).astype(F32)
    ar = rows[:, None] * freqs[None, :]
    ac = cols[:, None] * freqs[None, :]
    cos_t = jnp.concatenate([jnp.cos(ar), jnp.cos(ar), jnp.cos(ac), jnp.cos(ac)], axis=1)
    sin_t = jnp.concatenate([-jnp.sin(ar), jnp.sin(ar), -jnp.sin(ac), jnp.sin(ac)], axis=1)
    return cos_t, sin_t


def _route(logits, b_dummy=None):
    lg = logits[:, :N_GROUPS]
    le = logits[:, N_GROUPS:N_GROUPS + N_EXPERTS].reshape(-1, N_GROUPS, EXPERTS_PER_GROUP)
    p_group = jax.nn.softmax(lg, axis=-1)
    pg_top, g_top = lax.top_k(p_group, 1)
    sel = jnp.take_along_axis(le, g_top[:, :, None], axis=1)[:, 0]
    pe_top, e_local = lax.top_k(jax.nn.softmax(sel, axis=-1), TOP_K)
    return pg_top * pe_top, g_top * EXPERTS_PER_GROUP + e_local


def _dispatch(expert_ids, t):
    bm = MOE_ROWS
    n_assign = t * TOP_K
    e_flat = expert_ids.reshape(-1).astype(jnp.int32)
    tok_flat = jnp.repeat(jnp.arange(t, dtype=jnp.int32), TOP_K)
    order = jnp.argsort(e_flat)
    se = e_flat[order]
    counts = jnp.bincount(e_flat, length=N_EXPERTS)
    padded = ((counts + bm - 1) // bm) * bm
    start = jnp.cumsum(counts) - counts
    pend = jnp.cumsum(padded)
    pstart = pend - padded
    dest = (pstart[se] + jnp.arange(n_assign) - start[se]).astype(jnp.int32)
    n_blocks = -(-n_assign // bm) + N_EXPERTS
    row_tok = jnp.zeros((n_blocks * bm,), jnp.int32).at[dest].set(tok_flat[order])
    block_e = jnp.clip(jnp.searchsorted(pend, jnp.arange(n_blocks) * bm, side='right'), 0, N_EXPERTS - 1)
    pos = jnp.zeros((n_assign,), jnp.int32).at[order].set(dest)
    n_used = (pend[-1] // bm).astype(jnp.int32).reshape(1)
    return row_tok, block_e.astype(jnp.int32), pos, n_used


def kernel(x, c, ctx, c_ctx, w_mod, b_mod, g_pre_mix, g_post_mix, g_pre_ffn, g_post_ffn, w_in, ret_decay, pool_w, pool_scale, w_ret_out, w_pool_out, w_o, w_router_group, b_router_group, w_router_expert, b_router_expert, w_exp_gate, w_exp_up, w_exp_down):
    b, seq, d = x.shape
    lc = ctx.shape[1]
    heads = d // HEAD_DIM
    t = b * seq
    assert w_mod.shape[0] == 1, "single layer"
    x2 = x.reshape(t, d)

    pad = (-(b + 1)) % 8
    cc = jnp.concatenate([c, c_ctx[None, :], jnp.zeros((pad, d), F32)], axis=0)
    mod = _mod(cc, w_mod[0], b_mod[0][None, :])
    sh_m, sc_m, gt_m, sh_f, sc_f, gt_f = [mod[:b, k * d:(k + 1) * d].reshape(b, 1, d) for k in range(6)]
    csh_m = mod[b:b + 1, 0:d].reshape(1, 1, d)
    csc_m = mod[b:b + 1, d:2 * d].reshape(1, 1, d)
    log_gammas = -jnp.exp(ret_decay[0].astype(F32))

    w_in_b = w_in[0].astype(BF16)
    tn = min(1024, d)
    per_seg = d // tn
    proj = _inproj(x2, g_pre_mix[0][None, :], sh_m, sc_m, w_in_b, seq, 0, 7 * per_seg, tn)
    kvc = _inproj(ctx.reshape(b * lc, d), g_pre_mix[0][None, :], csh_m, csc_m, w_in_b, b * lc,
                  per_seg, 2 * per_seg, tn)
    proj3 = proj.reshape(b, seq, 7 * d)
    kvc3 = kvc.reshape(b, lc, 2 * d)

    cos_t, sin_t = _rope_tables(seq)
    y_ret = _retention(proj3, kvc3, log_gammas, cos_t, sin_t, heads).reshape(t, d)
    y_pool = _pool(proj3, 4, pool_w[0].astype(BF16), pool_scale[0][None, :]).reshape(t, d)

    merged = _merge(y_ret, y_pool, proj, 5, w_ret_out[0].astype(BF16), w_pool_out[0].astype(BF16))

    n_r = N_GROUPS + N_EXPERTS
    w_router = jnp.concatenate([w_router_group[0], w_router_expert[0],
                                jnp.zeros((d, ROUTER_LANES - n_r), F32)], axis=1).astype(BF16)
    b_router = jnp.concatenate([b_router_group[0], b_router_expert[0],
                                jnp.zeros((ROUTER_LANES - n_r,), F32)])[None, :]
    x1, h2, logits = _post(merged, x2, w_o[0].astype(BF16), g_post_mix[0][None, :], g_pre_ffn[0][None, :],
                           gt_m, sh_f, sc_f, w_router, b_router, seq)

    weights, expert_ids = _route(logits)
    row_tok, block_e, pos, n_used = _dispatch(expert_ids, t)
    xin = _gather_rows(h2, row_tok.reshape(-1, MOE_ROWS))
    yb = _experts(xin, block_e, n_used, w_exp_gate[0].astype(BF16), w_exp_up[0].astype(BF16),
                  w_exp_down[0].astype(BF16))
    tm_c = min(256, seq)
    out = _combine(yb, pos.reshape(-1, TOP_K * tm_c), weights, x1, g_post_ffn[0][None, :], gt_f, seq)
    return out.reshape(b, seq, d)
```

```python
import functools

import jax
import jax.numpy as jnp
import numpy as np
from jax import lax
from jax.experimental import pallas as pl
from jax.experimental.pallas import tpu as pltpu

F32 = jnp.float32
BF16 = jnp.bfloat16

HEAD_DIM = 256
GRID_W = 64
ROPE_BASE = 10000.0
POOL_WINDOWS = (2, 4, 8, 16)
N_GROUPS = 8
EXPERTS_PER_GROUP = 8
N_EXPERTS = N_GROUPS * EXPERTS_PER_GROUP
TOP_K = 2
EPS = 1e-6
KEY_SCALE = HEAD_DIM ** -0.5

RET_CHUNK = 256
MOE_ROWS = 256
ROUTER_LANES = 128
V7X_VMEM_LIMIT = 56 * 1024 * 1024


def _cparams(sem, vmem=V7X_VMEM_LIMIT):
    return pltpu.CompilerParams(dimension_semantics=sem, vmem_limit_bytes=vmem)


def _sigmoid(x):
    return 1.0 / (1.0 + jnp.exp(-x))


def _silu(x):
    return x * _sigmoid(x)


def _rms(x, g):
    return x * lax.rsqrt(jnp.mean(x * x, axis=-1, keepdims=True) + EPS) * g


def _mod_kernel(c_ref, w_ref, b_ref, o_ref):
    s = _silu(c_ref[...]).astype(BF16)
    o_ref[...] = jnp.dot(s, w_ref[...].astype(BF16), preferred_element_type=F32) + b_ref[...]


def _mod(cc, w_mod, b_mod):
    rows, d = cc.shape
    n = w_mod.shape[1]
    tn = min(1024, n)
    return pl.pallas_call(
        _mod_kernel,
        grid=(n // tn,),
        in_specs=[pl.BlockSpec((rows, d), lambda j: (0, 0)),
                  pl.BlockSpec((d, tn), lambda j: (0, j)),
                  pl.BlockSpec((1, tn), lambda j: (0, j))],
        out_specs=pl.BlockSpec((rows, tn), lambda j: (0, j)),
        out_shape=jax.ShapeDtypeStruct((rows, n), F32),
        compiler_params=_cparams(("parallel",)),
        name="mod",
    )(cc, w_mod, b_mod)


def _inproj_kernel(x_ref, g_ref, sh_ref, sc_ref, w_ref, o_ref, h_scr):
    @pl.when(pl.program_id(1) == 0)
    def _():
        y = _rms(x_ref[...], g_ref[...])
        h_scr[...] = (y * (1.0 + sc_ref[0]) + sh_ref[0]).astype(BF16)

    o_ref[...] = jnp.dot(h_scr[...], w_ref[...], preferred_element_type=F32).astype(o_ref.dtype)


def _inproj(x2, g, sh, sc, w, rows_per_mod, col_tile0, n_col_tiles, tn):
    m, d = x2.shape
    tm = min(1024, rows_per_mod, m)
    mod_of = (lambda i: (i * tm) // rows_per_mod) if sh.shape[0] > 1 else (lambda i: 0)
    return pl.pallas_call(
        _inproj_kernel,
        grid=(m // tm, n_col_tiles),
        in_specs=[pl.BlockSpec((tm, d), lambda i, j: (i, 0)),
                  pl.BlockSpec((1, d), lambda i, j: (0, 0)),
                  pl.BlockSpec((1, 1, d), lambda i, j: (mod_of(i), 0, 0)),
                  pl.BlockSpec((1, 1, d), lambda i, j: (mod_of(i), 0, 0)),
                  pl.BlockSpec((d, tn), lambda i, j: (0, j + col_tile0))],
        out_specs=pl.BlockSpec((tm, tn), lambda i, j: (i, j)),
        out_shape=jax.ShapeDtypeStruct((m, n_col_tiles * tn), BF16),
        scratch_shapes=[pltpu.VMEM((tm, d), BF16)],
        compiler_params=_cparams(("parallel", "arbitrary")),
        name="inproj",
    )(x2, g, sh, sc, w)


def _ret_kernel(lg_ref, q_ref, k_ref, v_ref, g_ref, kc_ref, vc_ref, cos_ref, sin_ref, o_ref,
                qs, ks, os_, sf, sb, dmat, dqf, dkf, dqb, dkb):
    h = pl.program_id(1)
    lgf = lg_ref[0, h]
    lgb = lg_ref[1, h]
    seq = q_ref.shape[0]
    lc = kc_ref.shape[0]
    c = RET_CHUNK
    n_chunks = seq // c
    half = HEAD_DIM // 2

    def rope(t_ref):
        t = t_ref[...].astype(F32)
        sw = jnp.concatenate([pltpu.roll(t[:, :half], half // 2, 1),
                              pltpu.roll(t[:, half:], half // 2, 1)], axis=1)
        return t * cos_ref[...] + sw * sin_ref[...]

    qs[...] = rope(q_ref).astype(BF16)
    ks[...] = (rope(k_ref) * KEY_SCALE).astype(BF16)

    ri = lax.broadcasted_iota(jnp.int32, (c, c), 0)
    ci = lax.broadcasted_iota(jnp.int32, (c, c), 1)
    diff = (ri - ci).astype(F32)
    dmat[...] = (jnp.where(diff >= 0, jnp.exp(lgf * jnp.maximum(diff, 0.0)), 0.0)
                 + jnp.where(diff <= 0, jnp.exp(lgb * jnp.maximum(-diff, 0.0)), 0.0))
    rows = lax.broadcasted_iota(jnp.int32, (c, HEAD_DIM), 0).astype(F32)
    dqf[...] = jnp.exp(lgf * (rows + 1.0))
    dkf[...] = jnp.exp(lgf * (c - 1.0 - rows))
    dqb[...] = jnp.exp(lgb * (c - rows))
    dkb[...] = jnp.exp(lgb * rows)
    gf_c = dqf[c - 1:c, :]
    gb_c = dqb[0:1, :]

    pos = lax.broadcasted_iota(jnp.int32, (lc, HEAD_DIM), 0).astype(F32)
    kc = kc_ref[...].astype(F32) * KEY_SCALE
    vc = vc_ref[...]
    tdot = functools.partial(lax.dot_general, dimension_numbers=(((0,), (0,)), ((), ())),
                             preferred_element_type=F32)
    sf[...] = tdot((kc * jnp.exp(lgf * (lc - 1.0 - pos))).astype(BF16), vc)
    sb[...] = tdot((kc * jnp.exp(lgb * pos)).astype(BF16), vc)

    def fwd(i, carry):
        r0 = pl.multiple_of(i * c, c)
        qc = qs[pl.ds(r0, c), :]
        kk = ks[pl.ds(r0, c), :]
        vv = v_ref[pl.ds(r0, c), :]
        s = lax.dot_general(qc, kk, (((1,), (1,)), ((), ())), preferred_element_type=F32)
        p = (s * dmat[...]).astype(BF16)
        st = sf[...]
        o = jnp.dot(p, vv, preferred_element_type=F32)
        o = o + jnp.dot(qc, st.astype(BF16), preferred_element_type=F32) * dqf[...]
        os_[pl.ds(r0, c), :] = o
        kd = (kk.astype(F32) * dkf[...]).astype(BF16)
        sf[...] = st * gf_c + tdot(kd, vv)
        return carry

    lax.fori_loop(0, n_chunks, fwd, 0)

    def bwd(j, carry):
        i = n_chunks - 1 - j
        r0 = pl.multiple_of(i * c, c)
        qc = qs[pl.ds(r0, c), :]
        kk = ks[pl.ds(r0, c), :]
        vv = v_ref[pl.ds(r0, c), :]
        st = sb[...]
        o = jnp.dot(qc, st.astype(BF16), preferred_element_type=F32) * dqb[...]
        os_[pl.ds(r0, c), :] = os_[pl.ds(r0, c), :] + o
        kd = (kk.astype(F32) * dkb[...]).astype(BF16)
        sb[...] = st * gb_c + tdot(kd, vv)
        return carry

    lax.fori_loop(0, n_chunks, bwd, 0)

    o = os_[...]
    o = o * lax.rsqrt(jnp.mean(o * o, axis=-1, keepdims=True) + EPS)
    o_ref[...] = (_silu(g_ref[...].astype(F32)) * o).astype(o_ref.dtype)


def _retention(proj3, kvc3, log_gammas, cos_t, sin_t, heads):
    b, seq, _ = proj3.shape
    lc = kvc3.shape[1]
    hd = HEAD_DIM
    blk = lambda off: pl.BlockSpec((None, seq, hd), lambda bi, hi: (bi, 0, off + hi))
    cblk = lambda off: pl.BlockSpec((None, lc, hd), lambda bi, hi: (bi, 0, off + hi))
    tab = pl.BlockSpec((seq, hd), lambda bi, hi: (0, 0))
    c = RET_CHUNK
    return pl.pallas_call(
        _ret_kernel,
        grid=(b, heads),
        in_specs=[pl.BlockSpec(memory_space=pltpu.SMEM),
                  blk(0), blk(heads), blk(2 * heads), blk(3 * heads),
                  cblk(0), cblk(heads), tab, tab],
        out_specs=pl.BlockSpec((None, seq, hd), lambda bi, hi: (bi, 0, hi)),
        out_shape=jax.ShapeDtypeStruct((b, seq, heads * hd), BF16),
        scratch_shapes=[pltpu.VMEM((seq, hd), BF16), pltpu.VMEM((seq, hd), BF16),
                        pltpu.VMEM((seq, hd), F32),
                        pltpu.VMEM((hd, hd), F32), pltpu.VMEM((hd, hd), F32),
                        pltpu.VMEM((c, c), F32),
                        pltpu.VMEM((c, hd), F32), pltpu.VMEM((c, hd), F32),
                        pltpu.VMEM((c, hd), F32), pltpu.VMEM((c, hd), F32)],
        compiler_params=_cparams(("parallel", "parallel")),
        name="retention",
    )(log_gammas, proj3, proj3, proj3, proj3, kvc3, kvc3, cos_t, sin_t)


def _pool_kernel(u_ref, bc_ref, cnt_ref, gw_ref, ps_ref, o_ref):
    seq = u_ref.shape[0]
    pg = gw_ref.shape[1]
    blk = bc_ref.shape[1]
    n_slab = seq // GRID_W
    for gi, w in enumerate(POOL_WINDOWS):
        lo = w // 2
        hi = w - lo - 1
        ub = u_ref[:, gi * pg:(gi + 1) * pg]
        s1 = jnp.concatenate(
            [jnp.dot(bc_ref[gi], ub[r * blk:(r + 1) * blk], preferred_element_type=F32)
             for r in range(seq // blk)], axis=0)
        zpad = lambda n: jnp.zeros((n * GRID_W, pg), F32)
        a = jnp.concatenate([zpad(lo), s1, zpad(hi)], axis=0) if hi else jnp.concatenate([zpad(lo), s1], axis=0)
        step = 1
        while step < w:
            n_rows = a.shape[0] - step * GRID_W
            a = a[:n_rows] + a[step * GRID_W:]
            step *= 2
        assert a.shape[0] == n_slab * GRID_W
        cnt = cnt_ref[gi]
        m = a / (cnt if pg == 128 else jnp.concatenate([cnt] * (pg // 128), axis=1))
        d = (m - ub.astype(F32)).astype(BF16)
        y = jnp.dot(d, gw_ref[gi], preferred_element_type=F32) * ps_ref[:, gi * pg:(gi + 1) * pg]
        o_ref[:, gi * pg:(gi + 1) * pg] = y.astype(o_ref.dtype)


def _pool_tables(seq, blk):
    rows = seq // GRID_W
    bands = np.zeros((len(POOL_WINDOWS), blk, blk), np.float32)
    cnts = np.zeros((len(POOL_WINDOWS), seq, 128), np.float32)
    t = np.arange(seq)
    r, c = t // GRID_W, t % GRID_W
    for gi, w in enumerate(POOL_WINDOWS):
        lo = w // 2
        hi = w - lo - 1
        col = np.arange(GRID_W)
        band = ((col[None, :] >= col[:, None] - lo) & (col[None, :] <= col[:, None] + hi)).astype(np.float32)
        bands[gi] = np.kron(np.eye(blk // GRID_W, dtype=np.float32), band)
        cc = np.minimum(c + hi + 1, GRID_W) - np.maximum(c - lo, 0)
        cr = np.minimum(r + hi + 1, rows) - np.maximum(r - lo, 0)
        cnts[gi] = (cc * cr).astype(np.float32)[:, None]
    return jnp.asarray(bands, BF16), jnp.asarray(cnts, F32)


def _pool(proj3, col_block, pool_w, pool_scale):
    b, seq, _ = proj3.shape
    n_g, pg, _ = pool_w.shape
    pw = n_g * pg
    blk = min(256, seq)
    bands, cnts = _pool_tables(seq, blk)
    return pl.pallas_call(
        _pool_kernel,
        grid=(b,),
        in_specs=[pl.BlockSpec((None, seq, pw), lambda bi: (bi, 0, col_block)),
                  pl.BlockSpec((n_g, blk, blk), lambda bi: (0, 0, 0)),
                  pl.BlockSpec((n_g, seq, 128), lambda bi: (0, 0, 0)),
                  pl.BlockSpec((n_g, pg, pg), lambda bi: (0, 0, 0)),
                  pl.BlockSpec((1, pw), lambda bi: (0, 0))],
        out_specs=pl.BlockSpec((None, seq, pw), lambda bi: (bi, 0, 0)),
        out_shape=jax.ShapeDtypeStruct((b, seq, pw), BF16),
        compiler_params=_cparams(("parallel",)),
        name="pool",
    )(proj3, bands, cnts, pool_w, pool_scale)


def _merge_kernel(yr_ref, yp_ref, gr_ref, gp_ref, wr_ref, wp_ref, o_ref):
    a = jnp.dot(yr_ref[...], wr_ref[...], preferred_element_type=F32)
    p = jnp.dot(yp_ref[...], wp_ref[...], preferred_element_type=F32)
    o_ref[...] = (_sigmoid(gr_ref[...].astype(F32)) * a + _sigmoid(gp_ref[...].astype(F32)) * p).astype(o_ref.dtype)


def _merge(y_ret, y_pool, proj, gate_col0, w_ret_out, w_pool_out):
    t, d = y_ret.shape
    tm = min(512, t)
    tn = min(1024, d)
    nj = d // tn
    return pl.pallas_call(
        _merge_kernel,
        grid=(t // tm, nj),
        in_specs=[pl.BlockSpec((tm, d), lambda i, j: (i, 0)),
                  pl.BlockSpec((tm, d), lambda i, j: (i, 0)),
                  pl.BlockSpec((tm, tn), lambda i, j: (i, gate_col0 * nj + j)),
                  pl.BlockSpec((tm, tn), lambda i, j: (i, (gate_col0 + 1) * nj + j)),
                  pl.BlockSpec((d, tn), lambda i, j: (0, j)),
                  pl.BlockSpec((d, tn), lambda i, j: (0, j))],
        out_specs=pl.BlockSpec((tm, tn), lambda i, j: (i, j)),
        out_shape=jax.ShapeDtypeStruct((t, d), BF16),
        compiler_params=_cparams(("parallel", "parallel")),
        name="merge",
    )(y_ret, y_pool, proj, proj, w_ret_out, w_pool_out)


NEG_BIG = -3.0e38
LANE_NONE = 1.0e9


def _route_block(logits):
    lane = lax.broadcasted_iota(jnp.int32, logits.shape, 1).astype(F32)

    def first_max(v):
        m = jnp.max(v, axis=1, keepdims=True)
        return m, jnp.min(jnp.where(v == m, lane, LANE_NONE), axis=1, keepdims=True)

    is_g = lane < N_GROUPS
    gmax, g_idx = first_max(jnp.where(is_g, logits, NEG_BIG))
    gden = jnp.sum(jnp.where(is_g, jnp.exp(logits - gmax), 0.0), axis=1, keepdims=True)
    lo = N_GROUPS + EXPERTS_PER_GROUP * g_idx
    in_grp = (lane >= lo) & (lane < lo + EXPERTS_PER_GROUP)
    le = jnp.where(in_grp, logits, NEG_BIG)
    m1, i1 = first_max(le)
    m2, i2 = first_max(jnp.where(lane == i1, NEG_BIG, le))
    eden = jnp.sum(jnp.where(in_grp, jnp.exp(logits - m1), 0.0), axis=1, keepdims=True)
    pg = 1.0 / gden
    w1 = pg * (1.0 / eden)
    w2 = pg * (jnp.exp(m2 - m1) / eden)
    return w1, w2, i1 - N_GROUPS, i2 - N_GROUPS


def _post_kernel(m_ref, x_ref, wo_ref, gpm_ref, gpf_ref, gt_ref, sh_ref, sc_ref, wr_ref, br_ref, tri_ref,
                 x1_ref, h2_ref, rt_ref, cnt_ref, carry):
    @pl.when(pl.program_id(0) == 0)
    def _():
        carry[...] = jnp.zeros(carry.shape, F32)

    mix = jnp.dot(m_ref[...], wo_ref[...], preferred_element_type=F32)
    x1 = x_ref[...] + gt_ref[0] * _rms(mix, gpm_ref[...])
    x1_ref[...] = x1
    h2 = _rms(x1, gpf_ref[...]) * (1.0 + sc_ref[0]) + sh_ref[0]
    h2_ref[...] = h2
    logits = jnp.dot(h2.astype(BF16), wr_ref[...], preferred_element_type=F32) + br_ref[...]
    w1, w2, e1, e2 = _route_block(logits)

    lane = lax.broadcasted_iota(jnp.int32, logits.shape, 1).astype(F32)
    oh1 = lane == e1
    oh2 = lane == e2
    both = jnp.where(oh1 | oh2, 1.0, 0.0)
    base = jnp.dot(tri_ref[...], both.astype(BF16), preferred_element_type=F32) + carry[0:1, :]
    r1 = jnp.sum(jnp.where(oh1, base, 0.0), axis=1, keepdims=True)
    r2 = jnp.sum(jnp.where(oh2, base, 0.0), axis=1, keepdims=True)
    new_carry = carry[...] + jnp.sum(both, axis=0, keepdims=True)
    carry[...] = new_carry
    cnt_ref[...] = new_carry

    cols = (w1, w2, e1, e2, r1, r2)
    out = jnp.zeros(logits.shape, F32)
    for k, col in enumerate(cols):
        out = jnp.where(lane == float(k), col, out)
    rt_ref[...] = out


def _post(merged, x2, w_o, g_post_mix, g_pre_ffn, gt_m, sh_f, sc_f, w_router, b_router, seq):
    t, d = x2.shape
    tm = min(256, seq)
    per_b = seq // tm
    vec = pl.BlockSpec((1, d), lambda i: (0, 0))
    bvec = pl.BlockSpec((1, 1, d), lambda i: (i // per_b, 0, 0))
    row = pl.BlockSpec((tm, d), lambda i: (i, 0))
    tri = jnp.asarray(np.tril(np.ones((tm, tm), np.float32), -1), BF16)
    return pl.pallas_call(
        _post_kernel,
        grid=(t // tm,),
        in_specs=[row, row,
                  pl.BlockSpec((d, d), lambda i: (0, 0)),
                  vec, vec, bvec, bvec, bvec,
                  pl.BlockSpec((d, ROUTER_LANES), lambda i: (0, 0)),
                  pl.BlockSpec((1, ROUTER_LANES), lambda i: (0, 0)),
                  pl.BlockSpec((tm, tm), lambda i: (0, 0))],
        out_specs=[row, row, pl.BlockSpec((tm, ROUTER_LANES), lambda i: (i, 0)),
                   pl.BlockSpec((8, ROUTER_LANES), lambda i: (0, 0))],
        out_shape=[jax.ShapeDtypeStruct((t, d), F32), jax.ShapeDtypeStruct((t, d), F32),
                   jax.ShapeDtypeStruct((t, ROUTER_LANES), F32),
                   jax.ShapeDtypeStruct((8, ROUTER_LANES), F32)],
        scratch_shapes=[pltpu.VMEM((8, ROUTER_LANES), F32)],
        compiler_params=_cparams(("arbitrary",)),
        name="post",
    )(merged, x2, w_o, g_post_mix, g_pre_ffn, gt_m, sh_f, sc_f, w_router, b_router, tri)


def _expert_kernel(be_ref, nu_ref, rt_ref, h2_ref, wg_ref, wu_ref, wd_ref, o_ref, xb0, xb1, sems):
    i = pl.program_id(0)
    n = pl.num_programs(0)
    nu = nu_ref[0]
    bm = xb0.shape[0]
    bufs = (xb0, xb1)

    def row_copy(step, r, s):
        return pltpu.make_async_copy(h2_ref.at[pl.ds(rt_ref[step, r], 1)], bufs[s].at[pl.ds(r, 1)], sems.at[s])

    def wait_block(s):
        pltpu.make_async_copy(h2_ref.at[pl.ds(0, bm)], bufs[s], sems.at[s]).wait()

    @pl.when(i == 0)
    def _():
        def body(r, carry):
            row_copy(0, r, 0).start()
            return carry
        lax.fori_loop(0, bm, body, 0)

    for s in range(2):
        @pl.when((i < nu) & (i % 2 == s))
        def _():
            wait_block(s)
            x = bufs[s][...].astype(BF16)
            nxt = jnp.minimum(i + 1, n - 1)
            for r in range(bm):
                row_copy(nxt, r, 1 - s).start()
            g = jnp.dot(x, wg_ref[...], preferred_element_type=F32)
            u = jnp.dot(x, wu_ref[...], preferred_element_type=F32)
            a = (_silu(g) * u).astype(BF16)
            o_ref[...] = jnp.dot(a, wd_ref[...], preferred_element_type=F32).astype(o_ref.dtype)

        @pl.when((i == nu) & (i % 2 == s))
        def _():
            wait_block(s)

        @pl.when((i == n - 1) & (i < nu) & (i % 2 == s))
        def _():
            wait_block(1 - s)

    @pl.when(i >= nu)
    def _():
        o_ref[...] = jnp.zeros(o_ref.shape, o_ref.dtype)


def _experts(h2, row_tok2, block_e, n_used, w_gate, w_up, w_down):
    n_blocks, bm = row_tok2.shape
    d = h2.shape[1]
    ff = w_gate.shape[2]
    return pl.pallas_call(
        _expert_kernel,
        grid_spec=pltpu.PrefetchScalarGridSpec(
            num_scalar_prefetch=3,
            grid=(n_blocks,),
            in_specs=[pl.BlockSpec(memory_space=pl.ANY),
                      pl.BlockSpec((None, d, ff), lambda i, be, nu, rt: (be[i], 0, 0)),
                      pl.BlockSpec((None, d, ff), lambda i, be, nu, rt: (be[i], 0, 0)),
                      pl.BlockSpec((None, ff, d), lambda i, be, nu, rt: (be[i], 0, 0))],
            out_specs=pl.BlockSpec((bm, d), lambda i, be, nu, rt: (i, 0)),
            scratch_shapes=[pltpu.VMEM((bm, d), F32), pltpu.VMEM((bm, d), F32),
                            pltpu.SemaphoreType.DMA((2,))]),
        out_shape=jax.ShapeDtypeStruct((n_blocks * bm, d), F32),
        compiler_params=_cparams(("arbitrary",)),
        name="experts",
    )(block_e, n_used, row_tok2, h2, w_gate, w_up, w_down)


def _combine_kernel(pos_ref, wt_ref, x1_ref, g_ref, gt_ref, yb_ref, o_ref, buf, sems):
    i = pl.program_id(0)
    n = pl.num_programs(0)
    tm = x1_ref.shape[0]
    slot = i % 2

    def issue(step, s):
        def body(r, carry):
            for kk in range(TOP_K):
                row = pos_ref[step, TOP_K * r + kk]
                pltpu.make_async_copy(yb_ref.at[pl.ds(row, 1)], buf.at[s, kk, pl.ds(r, 1)], sems.at[s]).start()
            return carry
        lax.fori_loop(0, tm, body, 0)

    @pl.when(i == 0)
    def _():
        issue(0, 0)

    @pl.when(i + 1 < n)
    def _():
        issue(i + 1, 1 - slot)

    for kk in range(TOP_K):
        pltpu.make_async_copy(yb_ref.at[pl.ds(0, tm)], buf.at[slot, kk], sems.at[slot]).wait()

    wt = wt_ref[...]
    ffn = buf[slot, 0] * wt[:, 0:1] + buf[slot, 1] * wt[:, 1:2]
    o_ref[...] = x1_ref[...] + gt_ref[0] * _rms(ffn, g_ref[...])


def _combine(yb, pos2, wts, x1, g_post_ffn, gt_f, seq):
    t, d = x1.shape
    tm = pos2.shape[1] // TOP_K
    per_b = seq // tm
    return pl.pallas_call(
        _combine_kernel,
        grid_spec=pltpu.PrefetchScalarGridSpec(
            num_scalar_prefetch=1,
            grid=(t // tm,),
            in_specs=[pl.BlockSpec((tm, TOP_K), lambda i, p: (i, 0)),
                      pl.BlockSpec((tm, d), lambda i, p: (i, 0)),
                      pl.BlockSpec((1, d), lambda i, p: (0, 0)),
                      pl.BlockSpec((1, 1, d), lambda i, p: (i // per_b, 0, 0)),
                      pl.BlockSpec(memory_space=pl.ANY)],
            out_specs=pl.BlockSpec((tm, d), lambda i, p: (i, 0)),
            scratch_shapes=[pltpu.VMEM((2, TOP_K, tm, d), F32), pltpu.SemaphoreType.DMA((2,))]),
        out_shape=jax.ShapeDtypeStruct((t, d), F32),
        compiler_params=_cparams(("arbitrary",)),
        name="combine",
    )(pos2, wts, x1, g_post_ffn, gt_f, yb)


def _rope_tables(seq):
    quarter = HEAD_DIM // 4
    freqs = ROPE_BASE ** (-jnp.arange(quarter, dtype=F32) / quarter)
    pos = jnp.arange(seq)
    rows = (pos // GRID_W).astype(F32)
    cols = (pos % GRID_W).astype(F32)
    ar = rows[:, None] * freqs[None, :]
    ac = cols[:, None] * freqs[None, :]
    cos_t = jnp.concatenate([jnp.cos(ar), jnp.cos(ar), jnp.cos(ac), jnp.cos(ac)], axis=1)
    sin_t = jnp.concatenate([-jnp.sin(ar), jnp.sin(ar), -jnp.sin(ac), jnp.sin(ac)], axis=1)
    return cos_t, sin_t


def _dispatch(route, counts, t):
    bm = MOE_ROWS
    n_assign = t * TOP_K
    e = route[:, 2:4].astype(jnp.int32)
    rank = route[:, 4:6].astype(jnp.int32)
    counts = counts.astype(jnp.int32)
    padded = ((counts + bm - 1) // bm) * bm
    pend = jnp.cumsum(padded)
    pstart = pend - padded
    onehot = e[:, :, None] == jnp.arange(N_EXPERTS, dtype=jnp.int32)[None, None, :]
    pos = (jnp.sum(jnp.where(onehot, pstart[None, None, :], 0), axis=-1) + rank).reshape(-1)
    n_blocks = -(-n_assign // bm) + N_EXPERTS
    tok_flat = jnp.repeat(jnp.arange(t, dtype=jnp.int32), TOP_K)
    row_tok = jnp.zeros((n_blocks * bm,), jnp.int32).at[pos].set(tok_flat)
    block_start = jnp.arange(n_blocks, dtype=jnp.int32) * bm
    block_e = jnp.minimum(jnp.sum(pend[None, :] <= block_start[:, None], axis=1), N_EXPERTS - 1)
    n_used = (pend[-1] // bm).astype(jnp.int32).reshape(1)
    return row_tok, block_e.astype(jnp.int32), pos, n_used


def kernel(x, c, ctx, c_ctx, w_mod, b_mod, g_pre_mix, g_post_mix, g_pre_ffn, g_post_ffn, w_in, ret_decay, pool_w, pool_scale, w_ret_out, w_pool_out, w_o, w_router_group, b_router_group, w_router_expert, b_router_expert, w_exp_gate, w_exp_up, w_exp_down):
    b, seq, d = x.shape
    lc = ctx.shape[1]
    heads = d // HEAD_DIM
    t = b * seq
    assert w_mod.shape[0] == 1, "single layer"
    x2 = x.reshape(t, d)

    pad = (-(b + 1)) % 8
    cc = jnp.concatenate([c, c_ctx[None, :], jnp.zeros((pad, d), F32)], axis=0)
    mod = _mod(cc, w_mod[0], b_mod[0][None, :])
    sh_m, sc_m, gt_m, sh_f, sc_f, gt_f = [mod[:b, k * d:(k + 1) * d].reshape(b, 1, d) for k in range(6)]
    csh_m = mod[b:b + 1, 0:d].reshape(1, 1, d)
    csc_m = mod[b:b + 1, d:2 * d].reshape(1, 1, d)
    log_gammas = -jnp.exp(ret_decay[0].astype(F32))

    w_in_b = w_in[0].astype(BF16)
    tn = min(1024, d)
    per_seg = d // tn
    proj = _inproj(x2, g_pre_mix[0][None, :], sh_m, sc_m, w_in_b, seq, 0, 7 * per_seg, tn)
    kvc = _inproj(ctx.reshape(b * lc, d), g_pre_mix[0][None, :], csh_m, csc_m, w_in_b, b * lc,
                  per_seg, 2 * per_seg, tn)
    proj3 = proj.reshape(b, seq, 7 * d)
    kvc3 = kvc.reshape(b, lc, 2 * d)

    cos_t, sin_t = _rope_tables(seq)
    y_ret = _retention(proj3, kvc3, log_gammas, cos_t, sin_t, heads).reshape(t, d)
    y_pool = _pool(proj3, 4, pool_w[0].astype(BF16), pool_scale[0][None, :]).reshape(t, d)

    merged = _merge(y_ret, y_pool, proj, 5, w_ret_out[0].astype(BF16), w_pool_out[0].astype(BF16))

    n_r = N_GROUPS + N_EXPERTS
    w_router = jnp.concatenate([w_router_group[0], w_router_expert[0],
                                jnp.zeros((d, ROUTER_LANES - n_r), F32)], axis=1).astype(BF16)
    b_router = jnp.concatenate([b_router_group[0], b_router_expert[0],
                                jnp.zeros((ROUTER_LANES - n_r,), F32)])[None, :]
    x1, h2, route, counts = _post(merged, x2, w_o[0].astype(BF16), g_post_mix[0][None, :],
                                  g_pre_ffn[0][None, :], gt_m, sh_f, sc_f, w_router, b_router, seq)

    row_tok, block_e, pos, n_used = _dispatch(route, counts[0, :N_EXPERTS], t)
    yb = _experts(h2, row_tok.reshape(-1, MOE_ROWS), block_e, n_used, w_exp_gate[0].astype(BF16),
                  w_exp_up[0].astype(BF16), w_exp_down[0].astype(BF16))
    tm_c = min(256, seq)
    out = _combine(yb, pos.reshape(-1, TOP_K * tm_c), route[:, 0:TOP_K], x1, g_post_ffn[0][None, :], gt_f, seq)
    return out.reshape(b, seq, d)
```

```python
import functools

import jax
import jax.numpy as jnp
import numpy as np
from jax import lax
from jax.experimental import pallas as pl
from jax.experimental.pallas import tpu as pltpu

F32 = jnp.float32
BF16 = jnp.bfloat16

HEAD_DIM = 256
GRID_W = 64
ROPE_BASE = 10000.0
POOL_WINDOWS = (2, 4, 8, 16)
N_GROUPS = 8
EXPERTS_PER_GROUP = 8
N_EXPERTS = N_GROUPS * EXPERTS_PER_GROUP
TOP_K = 2
EPS = 1e-6
KEY_SCALE = HEAD_DIM ** -0.5

RET_CHUNK = 256
MOE_ROWS = 256
LANE_TILE = 128
ROUTER_LANES = LANE_TILE
V7X_VMEM_LIMIT = 56 * 1024 * 1024


def _cparams(sem, vmem=V7X_VMEM_LIMIT):
    return pltpu.CompilerParams(dimension_semantics=sem, vmem_limit_bytes=vmem)


def _sigmoid(x):
    return 0.5 * jnp.tanh(0.5 * x) + 0.5


def _silu(x):
    return x * _sigmoid(x)


def _rms(x, g):
    return x * lax.rsqrt(jnp.mean(x * x, axis=-1, keepdims=True) + EPS) * g


def _mod_kernel(c_ref, w_ref, b_ref, o_ref):
    s = _silu(c_ref[...]).astype(BF16)
    o_ref[...] = jnp.dot(s, w_ref[...].astype(BF16), preferred_element_type=F32) + b_ref[...]


def _mod(cc, w_mod, b_mod):
    rows, d = cc.shape
    n = w_mod.shape[1]
    tn = min(1024, n)
    return pl.pallas_call(
        _mod_kernel,
        grid=(n // tn,),
        in_specs=[pl.BlockSpec((rows, d), lambda j: (0, 0)),
                  pl.BlockSpec((d, tn), lambda j: (0, j)),
                  pl.BlockSpec((1, tn), lambda j: (0, j))],
        out_specs=pl.BlockSpec((rows, tn), lambda j: (0, j)),
        out_shape=jax.ShapeDtypeStruct((rows, n), F32),
        compiler_params=_cparams(("parallel",)),
        name="mod",
    )(cc, w_mod, b_mod)


def _inproj_kernel(x_ref, g_ref, sh_ref, sc_ref, w_ref, o_ref, h_scr):
    @pl.when(pl.program_id(1) == 0)
    def _():
        y = _rms(x_ref[...], g_ref[...])
        h_scr[...] = (y * (1.0 + sc_ref[0]) + sh_ref[0]).astype(BF16)

    o_ref[...] = jnp.dot(h_scr[...], w_ref[...], preferred_element_type=F32).astype(o_ref.dtype)


def _inproj(x2, g, sh, sc, w, rows_per_mod, col_tile0, n_col_tiles, tn):
    m, d = x2.shape
    tm = min(1024, rows_per_mod, m)
    mod_of = (lambda i: (i * tm) // rows_per_mod) if sh.shape[0] > 1 else (lambda i: 0)
    return pl.pallas_call(
        _inproj_kernel,
        grid=(m // tm, n_col_tiles),
        in_specs=[pl.BlockSpec((tm, d), lambda i, j: (i, 0)),
                  pl.BlockSpec((1, d), lambda i, j: (0, 0)),
                  pl.BlockSpec((1, 1, d), lambda i, j: (mod_of(i), 0, 0)),
                  pl.BlockSpec((1, 1, d), lambda i, j: (mod_of(i), 0, 0)),
                  pl.BlockSpec((d, tn), lambda i, j: (0, j + col_tile0))],
        out_specs=pl.BlockSpec((tm, tn), lambda i, j: (i, j)),
        out_shape=jax.ShapeDtypeStruct((m, n_col_tiles * tn), BF16),
        scratch_shapes=[pltpu.VMEM((tm, d), BF16)],
        compiler_params=_cparams(("parallel", "arbitrary")),
        name="inproj",
    )(x2, g, sh, sc, w)


def _ret_kernel(lg_ref, q_ref, k_ref, v_ref, g_ref, kc_ref, vc_ref, cos_ref, sin_ref, swp_ref, o_ref,
                qs, ks, kdf, kdb, sbs, sf, sb, dmat, dqf, dkf, dqb, dkb):
    h = pl.program_id(0)
    seq = q_ref.shape[0]
    lc = kc_ref.shape[0]
    c = RET_CHUNK
    n_chunks = seq // c
    lgf = lg_ref[0, h]
    lgb = lg_ref[1, h]

    @pl.when(pl.program_id(1) == 0)
    def _():
        ri = lax.broadcasted_iota(jnp.int32, (c, c), 0)
        ci = lax.broadcasted_iota(jnp.int32, (c, c), 1)
        diff = (ri - ci).astype(F32)
        dmat[...] = (jnp.where(diff >= 0, jnp.exp(lgf * jnp.maximum(diff, 0.0)), 0.0)
                     + jnp.where(diff <= 0, jnp.exp(lgb * jnp.maximum(-diff, 0.0)), 0.0))
        rows = lax.broadcasted_iota(jnp.int32, (c, HEAD_DIM), 0).astype(F32)
        dqf[...] = jnp.exp(lgf * (rows + 1.0))
        dkf[...] = jnp.exp(lgf * (c - 1.0 - rows))
        dqb[...] = jnp.exp(lgb * (c - rows))
        dkb[...] = jnp.exp(lgb * rows)

    gf_c = dqf[c - 1:c, :]
    gb_c = dqb[0:1, :]

    def rope(t_ref):
        tb = t_ref[...]
        sw = jnp.dot(tb, swp_ref[...], preferred_element_type=F32)
        return tb.astype(F32) * cos_ref[...] + sw * sin_ref[...]

    qs[...] = rope(q_ref).astype(BF16)
    kr = (rope(k_ref) * KEY_SCALE).reshape(n_chunks, c, HEAD_DIM)
    ks[...] = kr.reshape(seq, HEAD_DIM).astype(BF16)
    kdf[...] = (kr * dkf[...][None]).reshape(seq, HEAD_DIM).astype(BF16)
    kdb[...] = (kr * dkb[...][None]).reshape(seq, HEAD_DIM).astype(BF16)

    pos = lax.broadcasted_iota(jnp.int32, (lc, HEAD_DIM), 0).astype(F32)
    kc = kc_ref[...].astype(F32) * KEY_SCALE
    vc = vc_ref[...]
    tdot = functools.partial(lax.dot_general, dimension_numbers=(((0,), (0,)), ((), ())),
                             preferred_element_type=F32)
    sf[...] = tdot((kc * jnp.exp(lgf * (lc - 1.0 - pos))).astype(BF16), vc)
    sb[...] = tdot((kc * jnp.exp(lgb * pos)).astype(BF16), vc)

    for i in reversed(range(n_chunks)):
        r = slice(i * c, (i + 1) * c)
        st = sb[...]
        sbs[i] = st.astype(BF16)
        if i > 0:
            sb[...] = st * gb_c + tdot(kdb[r, :], v_ref[r, :])

    for i in range(n_chunks):
        r = slice(i * c, (i + 1) * c)
        qc = qs[r, :]
        vv = v_ref[r, :]
        s = lax.dot_general(qc, ks[r, :], (((1,), (1,)), ((), ())), preferred_element_type=F32)
        st = sf[...]
        o = jnp.dot((s * dmat[...]).astype(BF16), vv, preferred_element_type=F32)
        o = o + jnp.dot(qc, st.astype(BF16), preferred_element_type=F32) * dqf[...]
        o = o + jnp.dot(qc, sbs[i], preferred_element_type=F32) * dqb[...]
        if i + 1 < n_chunks:
            sf[...] = st * gf_c + tdot(kdf[r, :], vv)
        o = o * lax.rsqrt(jnp.mean(o * o, axis=-1, keepdims=True) + EPS)
        o_ref[r, :] = (_silu(g_ref[r, :].astype(F32)) * o).astype(o_ref.dtype)


def _swap_matrix():
    half = HEAD_DIM // 2
    lane = np.arange(HEAD_DIM)
    partner = np.where(lane % half < half // 2, lane + half // 2, lane - half // 2)
    p = np.zeros((HEAD_DIM, HEAD_DIM), np.float32)
    p[partner, lane] = 1.0
    return jnp.asarray(p, BF16)


def _retention(proj3, kvc3, log_gammas, cos_t, sin_t, heads):
    b, seq, _ = proj3.shape
    lc = kvc3.shape[1]
    hd = HEAD_DIM
    blk = lambda off: pl.BlockSpec((None, seq, hd), lambda hi, bi: (bi, 0, off + hi))
    cblk = lambda off: pl.BlockSpec((None, lc, hd), lambda hi, bi: (bi, 0, off + hi))
    tab = pl.BlockSpec((seq, hd), lambda hi, bi: (0, 0))
    c = RET_CHUNK
    return pl.pallas_call(
        _ret_kernel,
        grid=(heads, b),
        in_specs=[pl.BlockSpec(memory_space=pltpu.SMEM),
                  blk(0), blk(heads), blk(2 * heads), blk(3 * heads),
                  cblk(0), cblk(heads), tab, tab,
                  pl.BlockSpec((hd, hd), lambda hi, bi: (0, 0))],
        out_specs=pl.BlockSpec((None, seq, hd), lambda hi, bi: (bi, 0, hi)),
        out_shape=jax.ShapeDtypeStruct((b, seq, heads * hd), BF16),
        scratch_shapes=[pltpu.VMEM((seq, hd), BF16), pltpu.VMEM((seq, hd), BF16),
                        pltpu.VMEM((seq, hd), BF16), pltpu.VMEM((seq, hd), BF16),
                        pltpu.VMEM((seq // c, hd, hd), BF16),
                        pltpu.VMEM((hd, hd), F32), pltpu.VMEM((hd, hd), F32),
                        pltpu.VMEM((c, c), F32),
                        pltpu.VMEM((c, hd), F32), pltpu.VMEM((c, hd), F32),
                        pltpu.VMEM((c, hd), F32), pltpu.VMEM((c, hd), F32)],
        compiler_params=_cparams(("arbitrary", "arbitrary")),
        name="retention",
    )(log_gammas, proj3, proj3, proj3, proj3, kvc3, kvc3, cos_t, sin_t, _swap_matrix())


def _pool_kernel(u_ref, bc_ref, cnt_ref, gw_ref, ps_ref, o_ref):
    seq = u_ref.shape[0]
    pg = gw_ref.shape[1]
    blk = bc_ref.shape[1]
    n_slab = seq // GRID_W
    for gi, w in enumerate(POOL_WINDOWS):
        lo = w // 2
        hi = w - lo - 1
        ub = u_ref[:, gi * pg:(gi + 1) * pg]
        s1 = jnp.concatenate(
            [jnp.dot(bc_ref[gi], ub[r * blk:(r + 1) * blk], preferred_element_type=F32)
             for r in range(seq // blk)], axis=0)
        zpad = lambda n: jnp.zeros((n * GRID_W, pg), F32)
        a = jnp.concatenate([zpad(lo), s1, zpad(hi)], axis=0) if hi else jnp.concatenate([zpad(lo), s1], axis=0)
        step = 1
        while step < w:
            n_rows = a.shape[0] - step * GRID_W
            a = a[:n_rows] + a[step * GRID_W:]
            step *= 2
        assert a.shape[0] == n_slab * GRID_W
        cnt = cnt_ref[gi]
        m = a / (cnt if pg == 128 else jnp.concatenate([cnt] * (pg // 128), axis=1))
        d = (m - ub.astype(F32)).astype(BF16)
        y = jnp.dot(d, gw_ref[gi], preferred_element_type=F32) * ps_ref[:, gi * pg:(gi + 1) * pg]
        o_ref[:, gi * pg:(gi + 1) * pg] = y.astype(o_ref.dtype)


def _pool_tables(seq, blk):
    rows = seq // GRID_W
    bands = np.zeros((len(POOL_WINDOWS), blk, blk), np.float32)
    cnts = np.zeros((len(POOL_WINDOWS), seq, 128), np.float32)
    t = np.arange(seq)
    r, c = t // GRID_W, t % GRID_W
    for gi, w in enumerate(POOL_WINDOWS):
        lo = w // 2
        hi = w - lo - 1
        col = np.arange(GRID_W)
        band = ((col[None, :] >= col[:, None] - lo) & (col[None, :] <= col[:, None] + hi)).astype(np.float32)
        bands[gi] = np.kron(np.eye(blk // GRID_W, dtype=np.float32), band)
        cc = np.minimum(c + hi + 1, GRID_W) - np.maximum(c - lo, 0)
        cr = np.minimum(r + hi + 1, rows) - np.maximum(r - lo, 0)
        cnts[gi] = (cc * cr).astype(np.float32)[:, None]
    return jnp.asarray(bands, BF16), jnp.asarray(cnts, F32)


def _pool(proj3, col_block, pool_w, pool_scale):
    b, seq, _ = proj3.shape
    n_g, pg, _ = pool_w.shape
    pw = n_g * pg
    blk = min(256, seq)
    bands, cnts = _pool_tables(seq, blk)
    return pl.pallas_call(
        _pool_kernel,
        grid=(b,),
        in_specs=[pl.BlockSpec((None, seq, pw), lambda bi: (bi, 0, col_block)),
                  pl.BlockSpec((n_g, blk, blk), lambda bi: (0, 0, 0)),
                  pl.BlockSpec((n_g, seq, 128), lambda bi: (0, 0, 0)),
                  pl.BlockSpec((n_g, pg, pg), lambda bi: (0, 0, 0)),
                  pl.BlockSpec((1, pw), lambda bi: (0, 0))],
        out_specs=pl.BlockSpec((None, seq, pw), lambda bi: (bi, 0, 0)),
        out_shape=jax.ShapeDtypeStruct((b, seq, pw), BF16),
        compiler_params=_cparams(("parallel",)),
        name="pool",
    )(proj3, bands, cnts, pool_w, pool_scale)


def _merge_kernel(yr_ref, yp_ref, gr_ref, gp_ref, wr_ref, wp_ref, o_ref):
    a = jnp.dot(yr_ref[...], wr_ref[...], preferred_element_type=F32)
    p = jnp.dot(yp_ref[...], wp_ref[...], preferred_element_type=F32)
    o_ref[...] = (_sigmoid(gr_ref[...].astype(F32)) * a + _sigmoid(gp_ref[...].astype(F32)) * p).astype(o_ref.dtype)


def _merge(y_ret, y_pool, proj, gate_col0, w_ret_out, w_pool_out):
    t, d = y_ret.shape
    tm = min(512, t)
    tn = min(1024, d)
    nj = d // tn
    return pl.pallas_call(
        _merge_kernel,
        grid=(t // tm, nj),
        in_specs=[pl.BlockSpec((tm, d), lambda i, j: (i, 0)),
                  pl.BlockSpec((tm, d), lambda i, j: (i, 0)),
                  pl.BlockSpec((tm, tn), lambda i, j: (i, gate_col0 * nj + j)),
                  pl.BlockSpec((tm, tn), lambda i, j: (i, (gate_col0 + 1) * nj + j)),
                  pl.BlockSpec((d, tn), lambda i, j: (0, j)),
                  pl.BlockSpec((d, tn), lambda i, j: (0, j))],
        out_specs=pl.BlockSpec((tm, tn), lambda i, j: (i, j)),
        out_shape=jax.ShapeDtypeStruct((t, d), BF16),
        compiler_params=_cparams(("parallel", "parallel")),
        name="merge",
    )(y_ret, y_pool, proj, proj, w_ret_out, w_pool_out)


NEG_BIG = -3.0e38
LANE_NONE = 1.0e9


def _route_block(logits):
    lane = lax.broadcasted_iota(jnp.int32, logits.shape, 1).astype(F32)

    def first_max(v):
        m = jnp.max(v, axis=1, keepdims=True)
        return m, jnp.min(jnp.where(v == m, lane, LANE_NONE), axis=1, keepdims=True)

    is_g = lane < N_GROUPS
    gmax, g_idx = first_max(jnp.where(is_g, logits, NEG_BIG))
    gden = jnp.sum(jnp.where(is_g, jnp.exp(logits - gmax), 0.0), axis=1, keepdims=True)
    lo = N_GROUPS + EXPERTS_PER_GROUP * g_idx
    in_grp = (lane >= lo) & (lane < lo + EXPERTS_PER_GROUP)
    le = jnp.where(in_grp, logits, NEG_BIG)
    m1, i1 = first_max(le)
    m2, i2 = first_max(jnp.where(lane == i1, NEG_BIG, le))
    eden = jnp.sum(jnp.where(in_grp, jnp.exp(logits - m1), 0.0), axis=1, keepdims=True)
    pg = 1.0 / gden
    w1 = pg * (1.0 / eden)
    w2 = pg * (jnp.exp(m2 - m1) / eden)
    return w1, w2, i1 - N_GROUPS, i2 - N_GROUPS


def _post_kernel(m_ref, x_ref, wo_ref, gpm_ref, gpf_ref, gt_ref, sh_ref, sc_ref, wr_ref, br_ref, tri_ref,
                 x1_ref, h2_ref, rt_ref, cnt_ref, carry):
    @pl.when(pl.program_id(0) == 0)
    def _():
        carry[...] = jnp.zeros(carry.shape, F32)

    mix = jnp.dot(m_ref[...], wo_ref[...], preferred_element_type=F32)
    x1 = x_ref[...] + gt_ref[0] * _rms(mix, gpm_ref[...])
    x1_ref[...] = x1
    h2 = _rms(x1, gpf_ref[...]) * (1.0 + sc_ref[0]) + sh_ref[0]
    h2_ref[...] = h2
    logits = jnp.dot(h2.astype(BF16), wr_ref[...], preferred_element_type=F32) + br_ref[...]
    w1, w2, e1, e2 = _route_block(logits)

    lane = lax.broadcasted_iota(jnp.int32, logits.shape, 1).astype(F32)
    oh1 = lane == e1
    oh2 = lane == e2
    both = jnp.where(oh1 | oh2, 1.0, 0.0)
    base = jnp.dot(tri_ref[...], both.astype(BF16), preferred_element_type=F32) + carry[0:1, :]
    r1 = jnp.sum(jnp.where(oh1, base, 0.0), axis=1, keepdims=True)
    r2 = jnp.sum(jnp.where(oh2, base, 0.0), axis=1, keepdims=True)
    new_carry = carry[...] + jnp.sum(both, axis=0, keepdims=True)
    carry[...] = new_carry
    cnt_ref[...] = new_carry

    cols = (w1, w2, e1, e2, r1, r2)
    out = jnp.zeros(logits.shape, F32)
    for k, col in enumerate(cols):
        out = jnp.where(lane == float(k), col, out)
    rt_ref[...] = out


def _post(merged, x2, w_o, g_post_mix, g_pre_ffn, gt_m, sh_f, sc_f, w_router, b_router, seq):
    t, d = x2.shape
    tm = min(256, seq)
    per_b = seq // tm
    vec = pl.BlockSpec((1, d), lambda i: (0, 0))
    bvec = pl.BlockSpec((1, 1, d), lambda i: (i // per_b, 0, 0))
    row = pl.BlockSpec((tm, d), lambda i: (i, 0))
    tri = jnp.asarray(np.tril(np.ones((tm, tm), np.float32), -1), BF16)
    return pl.pallas_call(
        _post_kernel,
        grid=(t // tm,),
        in_specs=[row, row,
                  pl.BlockSpec((d, d), lambda i: (0, 0)),
                  vec, vec, bvec, bvec, bvec,
                  pl.BlockSpec((d, ROUTER_LANES), lambda i: (0, 0)),
                  pl.BlockSpec((1, ROUTER_LANES), lambda i: (0, 0)),
                  pl.BlockSpec((tm, tm), lambda i: (0, 0))],
        out_specs=[row, row, pl.BlockSpec((tm, ROUTER_LANES), lambda i: (i, 0)),
                   pl.BlockSpec((8, ROUTER_LANES), lambda i: (0, 0))],
        out_shape=[jax.ShapeDtypeStruct((t, d), F32), jax.ShapeDtypeStruct((t, d), F32),
                   jax.ShapeDtypeStruct((t, ROUTER_LANES), F32),
                   jax.ShapeDtypeStruct((8, ROUTER_LANES), F32)],
        scratch_shapes=[pltpu.VMEM((8, ROUTER_LANES), F32)],
        compiler_params=_cparams(("arbitrary",)),
        name="post",
    )(merged, x2, w_o, g_post_mix, g_pre_ffn, gt_m, sh_f, sc_f, w_router, b_router, tri)


def _expert_kernel(be_ref, nu_ref, rt_ref, h2_ref, wg_ref, wu_ref, wd_ref, o_ref, xb0, xb1, sems):
    i = pl.program_id(0)
    n = pl.num_programs(0)
    nu = nu_ref[0]
    bm, d = xb0.shape
    nl = d // LANE_TILE
    bufs = (xb0, xb1)

    def row_copy(step, r, s):
        return pltpu.make_async_copy(h2_ref.at[pl.ds(rt_ref[step, r], 1)], bufs[s].at[pl.ds(r, 1)], sems.at[s])

    def wait_block(s):
        pltpu.make_async_copy(h2_ref.at[pl.ds(0, bm)], bufs[s], sems.at[s]).wait()

    @pl.when(i == 0)
    def _():
        def body(r, carry):
            row_copy(0, r, 0).start()
            return carry
        lax.fori_loop(0, bm, body, 0)

    for s in range(2):
        @pl.when((i < nu) & (i % 2 == s))
        def _():
            nxt = jnp.minimum(i + 1, n - 1)
            for r in range(bm):
                row_copy(nxt, r, 1 - s).start()
            wait_block(s)
            x = bufs[s][...].astype(BF16)
            g = jnp.dot(x, wg_ref[...], preferred_element_type=F32)
            u = jnp.dot(x, wu_ref[...], preferred_element_type=F32)
            a = (_silu(g) * u).astype(BF16)
            y = jnp.dot(a, wd_ref[...], preferred_element_type=F32)
            for j in range(nl):
                o_ref[pl.ds(j, bm, stride=nl), :] = y[:, j * LANE_TILE:(j + 1) * LANE_TILE]

        @pl.when((i == nu) & (i % 2 == s))
        def _():
            wait_block(s)

        @pl.when((i == n - 1) & (i < nu) & (i % 2 == s))
        def _():
            wait_block(1 - s)

    @pl.when(i >= nu)
    def _():
        o_ref[...] = jnp.zeros(o_ref.shape, o_ref.dtype)


def _experts(h2, row_tok2, block_e, n_used, w_gate, w_up, w_down):
    n_blocks, bm = row_tok2.shape
    d = h2.shape[1]
    nl = d // LANE_TILE
    ff = w_gate.shape[2]
    return pl.pallas_call(
        _expert_kernel,
        grid_spec=pltpu.PrefetchScalarGridSpec(
            num_scalar_prefetch=3,
            grid=(n_blocks,),
            in_specs=[pl.BlockSpec(memory_space=pl.ANY),
                      pl.BlockSpec((None, d, ff), lambda i, be, nu, rt: (be[i], 0, 0)),
                      pl.BlockSpec((None, d, ff), lambda i, be, nu, rt: (be[i], 0, 0)),
                      pl.BlockSpec((None, ff, d), lambda i, be, nu, rt: (be[i], 0, 0))],
            out_specs=pl.BlockSpec((bm * nl, LANE_TILE), lambda i, be, nu, rt: (i, 0)),
            scratch_shapes=[pltpu.VMEM((bm, d), F32), pltpu.VMEM((bm, d), F32),
                            pltpu.SemaphoreType.DMA((2,))]),
        out_shape=jax.ShapeDtypeStruct((n_blocks * bm * nl, LANE_TILE), F32),
        compiler_params=_cparams(("arbitrary",)),
        name="experts",
    )(block_e, n_used, row_tok2, h2, w_gate, w_up, w_down)


def _combine_kernel(pos_ref, wt_ref, x1_ref, g_ref, gt_ref, yb_ref, o_ref, buf, sems):
    i = pl.program_id(0)
    n = pl.num_programs(0)
    tm, d = x1_ref.shape
    nl = d // LANE_TILE
    slot = i % 2

    def issue(step, s):
        def body(r, carry):
            for kk in range(TOP_K):
                row = pl.multiple_of(pos_ref[step, TOP_K * r + kk] * nl, nl)
                dst = pl.multiple_of(r * nl, nl)
                pltpu.make_async_copy(yb_ref.at[pl.ds(row, nl)], buf.at[s, kk, pl.ds(dst, nl)], sems.at[s]).start()
            return carry
        lax.fori_loop(0, tm, body, 0)

    @pl.when(i == 0)
    def _():
        issue(0, 0)

    @pl.when(i + 1 < n)
    def _():
        issue(i + 1, 1 - slot)

    for kk in range(TOP_K):
        pltpu.make_async_copy(yb_ref.at[pl.ds(0, tm * nl)], buf.at[slot, kk], sems.at[slot]).wait()

    def rows(kk):
        view = buf.at[slot, kk]
        return jnp.concatenate([view[pl.ds(j, tm, stride=nl), :] for j in range(nl)], axis=1)

    wt = wt_ref[...]
    ffn = rows(0) * wt[:, 0:1] + rows(1) * wt[:, 1:2]
    o_ref[...] = x1_ref[...] + gt_ref[0] * _rms(ffn, g_ref[...])


def _combine(yb, pos2, wts, x1, g_post_ffn, gt_f, seq):
    t, d = x1.shape
    tm = pos2.shape[1] // TOP_K
    per_b = seq // tm
    return pl.pallas_call(
        _combine_kernel,
        grid_spec=pltpu.PrefetchScalarGridSpec(
            num_scalar_prefetch=1,
            grid=(t // tm,),
            in_specs=[pl.BlockSpec((tm, TOP_K), lambda i, p: (i, 0)),
                      pl.BlockSpec((tm, d), lambda i, p: (i, 0)),
                      pl.BlockSpec((1, d), lambda i, p: (0, 0)),
                      pl.BlockSpec((1, 1, d), lambda i, p: (i // per_b, 0, 0)),
                      pl.BlockSpec(memory_space=pl.ANY)],
            out_specs=pl.BlockSpec((tm, d), lambda i, p: (i, 0)),
            scratch_shapes=[pltpu.VMEM((2, TOP_K, tm * (d // LANE_TILE), LANE_TILE), F32),
                            pltpu.SemaphoreType.DMA((2,))]),
        out_shape=jax.ShapeDtypeStruct((t, d), F32),
        compiler_params=_cparams(("arbitrary",)),
        name="combine",
    )(pos2, wts, x1, g_post_ffn, gt_f, yb)


def _rope_tables(seq):
    quarter = HEAD_DIM // 4
    freqs = ROPE_BASE ** (-jnp.arange(quarter, dtype=F32) / quarter)
    pos = jnp.arange(seq)
    rows = (pos // GRID_W).astype(F32)
    cols = (pos % GRID_W).astype(F32)
    ar = rows[:, None] * freqs[None, :]
    ac = cols[:, None] * freqs[None, :]
    cos_t = jnp.concatenate([jnp.cos(ar), jnp.cos(ar), jnp.cos(ac), jnp.cos(ac)], axis=1)
    sin_t = jnp.concatenate([-jnp.sin(ar), jnp.sin(ar), -jnp.sin(ac), jnp.sin(ac)], axis=1)
    return cos_t, sin_t


def _dispatch(route, counts, t):
    bm = MOE_ROWS
    n_assign = t * TOP_K
    e = route[:, 2:4].astype(jnp.int32)
    rank = route[:, 4:6].astype(jnp.int32)
    counts = counts.astype(jnp.int32)
    padded = ((counts + bm - 1) // bm) * bm
    pend = jnp.cumsum(padded)
    pstart = pend - padded
    onehot = e[:, :, None] == jnp.arange(N_EXPERTS, dtype=jnp.int32)[None, None, :]
    pos = (jnp.sum(jnp.where(onehot, pstart[None, None, :], 0), axis=-1) + rank).reshape(-1)
    n_blocks = -(-n_assign // bm) + N_EXPERTS
    tok_flat = jnp.repeat(jnp.arange(t, dtype=jnp.int32), TOP_K)
    row_tok = jnp.zeros((n_blocks * bm,), jnp.int32).at[pos].set(tok_flat)
    block_start = jnp.arange(n_blocks, dtype=jnp.int32) * bm
    block_e = jnp.minimum(jnp.sum(pend[None, :] <= block_start[:, None], axis=1), N_EXPERTS - 1)
    n_used = (pend[-1] // bm).astype(jnp.int32).reshape(1)
    return row_tok, block_e.astype(jnp.int32), pos, n_used


def kernel(x, c, ctx, c_ctx, w_mod, b_mod, g_pre_mix, g_post_mix, g_pre_ffn, g_post_ffn, w_in, ret_decay, pool_w, pool_scale, w_ret_out, w_pool_out, w_o, w_router_group, b_router_group, w_router_expert, b_router_expert, w_exp_gate, w_exp_up, w_exp_down):
    b, seq, d = x.shape
    lc = ctx.shape[1]
    heads = d // HEAD_DIM
    t = b * seq
    assert w_mod.shape[0] == 1, "single layer"
    x2 = x.reshape(t, d)

    pad = (-(b + 1)) % 8
    cc = jnp.concatenate([c, c_ctx[None, :], jnp.zeros((pad, d), F32)], axis=0)
    mod = _mod(cc, w_mod[0], b_mod[0][None, :])
    sh_m, sc_m, gt_m, sh_f, sc_f, gt_f = [mod[:b, k * d:(k + 1) * d].reshape(b, 1, d) for k in range(6)]
    csh_m = mod[b:b + 1, 0:d].reshape(1, 1, d)
    csc_m = mod[b:b + 1, d:2 * d].reshape(1, 1, d)
    log_gammas = -jnp.exp(ret_decay[0].astype(F32))

    w_in_b = w_in[0].astype(BF16)
    tn = min(1024, d)
    per_seg = d // tn
    proj = _inproj(x2, g_pre_mix[0][None, :], sh_m, sc_m, w_in_b, seq, 0, 7 * per_seg, tn)
    kvc = _inproj(ctx.reshape(b * lc, d), g_pre_mix[0][None, :], csh_m, csc_m, w_in_b, b * lc,
                  per_seg, 2 * per_seg, tn)
    proj3 = proj.reshape(b, seq, 7 * d)
    kvc3 = kvc.reshape(b, lc, 2 * d)

    cos_t, sin_t = _rope_tables(seq)
    y_ret = _retention(proj3, kvc3, log_gammas, cos_t, sin_t, heads).reshape(t, d)
    y_pool = _pool(proj3, 4, pool_w[0].astype(BF16), pool_scale[0][None, :]).reshape(t, d)

    merged = _merge(y_ret, y_pool, proj, 5, w_ret_out[0].astype(BF16), w_pool_out[0].astype(BF16))

    n_r = N_GROUPS + N_EXPERTS
    w_router = jnp.concatenate([w_router_group[0], w_router_expert[0],
                                jnp.zeros((d, ROUTER_LANES - n_r), F32)], axis=1).astype(BF16)
    b_router = jnp.concatenate([b_router_group[0], b_router_expert[0],
                                jnp.zeros((ROUTER_LANES - n_r,), F32)])[None, :]
    x1, h2, route, counts = _post(merged, x2, w_o[0].astype(BF16), g_post_mix[0][None, :],
                                  g_pre_ffn[0][None, :], gt_m, sh_f, sc_f, w_router, b_router, seq)

    row_tok, block_e, pos, n_used = _dispatch(route, counts[0, :N_EXPERTS], t)
    yb = _experts(h2, row_tok.reshape(-1, MOE_ROWS), block_e, n_used, w_exp_gate[0].astype(BF16),
                  w_exp_up[0].astype(BF16), w_exp_down[0].astype(BF16))
    tm_c = min(256, seq)
    out = _combine(yb, pos.reshape(-1, TOP_K * tm_c), route[:, 0:TOP_K], x1, g_post_ffn[0][None, :], gt_f, seq)
    return out.reshape(b, seq, d)
```

```python
import functools

import jax
import jax.numpy as jnp
import numpy as np
from jax import lax
from jax.experimental import pallas as pl
from jax.experimental.pallas import tpu as pltpu

F32 = jnp.float32
BF16 = jnp.bfloat16

HEAD_DIM = 256
GRID_W = 64
ROPE_BASE = 10000.0
POOL_WINDOWS = (2, 4, 8, 16)
N_GROUPS = 8
EXPERTS_PER_GROUP = 8
N_EXPERTS = N_GROUPS * EXPERTS_PER_GROUP
TOP_K = 2
EPS = 1e-6
KEY_SCALE = HEAD_DIM ** -0.5

RET_CHUNK = 256
MOE_ROWS = 256
COMBINE_SLOTS = 3
EXPERT_SLOTS = 3
LANE_TILE = 128
ROUTER_LANES = LANE_TILE
V7X_VMEM_LIMIT = 56 * 1024 * 1024


def _cparams(sem, vmem=V7X_VMEM_LIMIT):
    return pltpu.CompilerParams(dimension_semantics=sem, vmem_limit_bytes=vmem)


def _sigmoid(x):
    return 0.5 * jnp.tanh(0.5 * x) + 0.5


def _silu(x):
    return x * _sigmoid(x)


def _rms(x, g):
    return x * lax.rsqrt(jnp.mean(x * x, axis=-1, keepdims=True) + EPS) * g


def _mod_kernel(c_ref, w_ref, b_ref, o_ref):
    s = _silu(c_ref[...]).astype(BF16)
    o_ref[...] = jnp.dot(s, w_ref[...].astype(BF16), preferred_element_type=F32) + b_ref[...]


def _mod(cc, w_mod, b_mod):
    rows, d = cc.shape
    n = w_mod.shape[1]
    tn = min(1024, n)
    return pl.pallas_call(
        _mod_kernel,
        grid=(n // tn,),
        in_specs=[pl.BlockSpec((rows, d), lambda j: (0, 0)),
                  pl.BlockSpec((d, tn), lambda j: (0, j)),
                  pl.BlockSpec((1, tn), lambda j: (0, j))],
        out_specs=pl.BlockSpec((rows, tn), lambda j: (0, j)),
        out_shape=jax.ShapeDtypeStruct((rows, n), F32),
        compiler_params=_cparams(("parallel",)),
        name="mod",
    )(cc, w_mod, b_mod)


def _inproj_kernel(n_cast, x_ref, g_ref, sh_ref, sc_ref, w_ref, *rest):
    cast_in, o_ref, cast_out, h_scr = rest[:n_cast], rest[n_cast], rest[n_cast + 1:-1], rest[-1]

    @pl.when(pl.program_id(1) == 0)
    def _():
        y = _rms(x_ref[...], g_ref[...])
        h_scr[...] = (y * (1.0 + sc_ref[0]) + sh_ref[0]).astype(BF16)

    o_ref[...] = jnp.dot(h_scr[...], w_ref[...], preferred_element_type=F32).astype(o_ref.dtype)
    for src, dst in zip(cast_in, cast_out):
        dst[...] = src[...].astype(BF16)


def _inproj(x2, g, sh, sc, w, rows_per_mod, col_tile0, n_col_tiles, tn, cast=()):
    m, d = x2.shape
    tm = min(1024, rows_per_mod, m)
    ni = m // tm
    js = 1 << (n_col_tiles.bit_length() - 1)
    mod_of = (lambda i: (i * tm) // rows_per_mod) if sh.shape[0] > 1 else (lambda i: 0)
    chunk = lambda a: pl.BlockSpec((a.shape[0] // (ni * js), a.shape[1]),
                                   lambda i, j: (i * js + jnp.minimum(j, js - 1), 0))
    outs = pl.pallas_call(
        functools.partial(_inproj_kernel, len(cast)),
        grid=(ni, n_col_tiles),
        in_specs=[pl.BlockSpec((tm, d), lambda i, j: (i, 0)),
                  pl.BlockSpec((1, d), lambda i, j: (0, 0)),
                  pl.BlockSpec((1, 1, d), lambda i, j: (mod_of(i), 0, 0)),
                  pl.BlockSpec((1, 1, d), lambda i, j: (mod_of(i), 0, 0)),
                  pl.BlockSpec((d, tn), lambda i, j: (0, j + col_tile0))] + [chunk(a) for a in cast],
        out_specs=[pl.BlockSpec((tm, tn), lambda i, j: (i, j))] + [chunk(a) for a in cast],
        out_shape=[jax.ShapeDtypeStruct((m, n_col_tiles * tn), BF16)]
                  + [jax.ShapeDtypeStruct(a.shape, BF16) for a in cast],
        scratch_shapes=[pltpu.VMEM((tm, d), BF16)],
        compiler_params=_cparams(("parallel", "arbitrary")),
        name="inproj",
    )(x2, g, sh, sc, w, *cast)
    return outs[0], tuple(outs[1:])


def _ret_kernel(lg_ref, q_ref, k_ref, v_ref, g_ref, kc_ref, vc_ref, cos_ref, sin_ref, swp_ref, wsrc_ref,
                o_ref, wdst_ref, qs, ks, kdf, kdb, sbs, sf, sb, dmat, dqf, dkf, dqb, dkb):
    h = pl.program_id(0)
    seq = q_ref.shape[0]
    lc = kc_ref.shape[0]
    c = RET_CHUNK
    n_chunks = seq // c
    lgf = lg_ref[0, h]
    lgb = lg_ref[1, h]

    wdst_ref[...] = wsrc_ref[...].astype(BF16)

    @pl.when(pl.program_id(1) == 0)
    def _():
        ri = lax.broadcasted_iota(jnp.int32, (c, c), 0)
        ci = lax.broadcasted_iota(jnp.int32, (c, c), 1)
        diff = (ri - ci).astype(F32)
        dmat[...] = (jnp.where(diff >= 0, jnp.exp(lgf * jnp.maximum(diff, 0.0)), 0.0)
                     + jnp.where(diff <= 0, jnp.exp(lgb * jnp.maximum(-diff, 0.0)), 0.0))
        rows = lax.broadcasted_iota(jnp.int32, (c, HEAD_DIM), 0).astype(F32)
        dqf[...] = jnp.exp(lgf * (rows + 1.0))
        dkf[...] = jnp.exp(lgf * (c - 1.0 - rows))
        dqb[...] = jnp.exp(lgb * (c - rows))
        dkb[...] = jnp.exp(lgb * rows)

    gf_c = dqf[c - 1:c, :]
    gb_c = dqb[0:1, :]

    def rope(t_ref):
        tb = t_ref[...]
        sw = jnp.dot(tb, swp_ref[...], preferred_element_type=F32)
        return tb.astype(F32) * cos_ref[...] + sw * sin_ref[...]

    qs[...] = rope(q_ref).astype(BF16)
    kr = (rope(k_ref) * KEY_SCALE).reshape(n_chunks, c, HEAD_DIM)
    ks[...] = kr.reshape(seq, HEAD_DIM).astype(BF16)
    kdf[...] = (kr * dkf[...][None]).reshape(seq, HEAD_DIM).astype(BF16)
    kdb[...] = (kr * dkb[...][None]).reshape(seq, HEAD_DIM).astype(BF16)

    pos = lax.broadcasted_iota(jnp.int32, (lc, HEAD_DIM), 0).astype(F32)
    kc = kc_ref[...].astype(F32) * KEY_SCALE
    vc = vc_ref[...]
    tdot = functools.partial(lax.dot_general, dimension_numbers=(((0,), (0,)), ((), ())),
                             preferred_element_type=F32)
    sf[...] = tdot((kc * jnp.exp(lgf * (lc - 1.0 - pos))).astype(BF16), vc)
    sb[...] = tdot((kc * jnp.exp(lgb * pos)).astype(BF16), vc)

    for i in reversed(range(n_chunks)):
        r = slice(i * c, (i + 1) * c)
        st = sb[...]
        sbs[i] = st.astype(BF16)
        if i > 0:
            sb[...] = st * gb_c + tdot(kdb[r, :], v_ref[r, :])

    for i in range(n_chunks):
        r = slice(i * c, (i + 1) * c)
        qc = qs[r, :]
        vv = v_ref[r, :]
        s = lax.dot_general(qc, ks[r, :], (((1,), (1,)), ((), ())), preferred_element_type=F32)
        st = sf[...]
        o = jnp.dot((s * dmat[...]).astype(BF16), vv, preferred_element_type=F32)
        o = o + jnp.dot(qc, st.astype(BF16), preferred_element_type=F32) * dqf[...]
        o = o + jnp.dot(qc, sbs[i], preferred_element_type=F32) * dqb[...]
        if i + 1 < n_chunks:
            sf[...] = st * gf_c + tdot(kdf[r, :], vv)
        o = o * lax.rsqrt(jnp.mean(o * o, axis=-1, keepdims=True) + EPS)
        o_ref[r, :] = (_silu(g_ref[r, :].astype(F32)) * o).astype(o_ref.dtype)


def _swap_matrix():
    half = HEAD_DIM // 2
    lane = np.arange(HEAD_DIM)
    partner = np.where(lane % half < half // 2, lane + half // 2, lane - half // 2)
    p = np.zeros((HEAD_DIM, HEAD_DIM), np.float32)
    p[partner, lane] = 1.0
    return jnp.asarray(p, BF16)


def _retention(proj3, kvc3, log_gammas, cos_t, sin_t, heads, cast):
    b, seq, _ = proj3.shape
    lc = kvc3.shape[1]
    hd = HEAD_DIM
    blk = lambda off: pl.BlockSpec((None, seq, hd), lambda hi, bi: (bi, 0, off + hi))
    cblk = lambda off: pl.BlockSpec((None, lc, hd), lambda hi, bi: (bi, 0, off + hi))
    tab = pl.BlockSpec((seq, hd), lambda hi, bi: (0, 0))
    chunk = pl.BlockSpec((cast.shape[0] // (heads * b), cast.shape[1]), lambda hi, bi: (hi * b + bi, 0))
    c = RET_CHUNK
    return pl.pallas_call(
        _ret_kernel,
        grid=(heads, b),
        in_specs=[pl.BlockSpec(memory_space=pltpu.SMEM),
                  blk(0), blk(heads), blk(2 * heads), blk(3 * heads),
                  cblk(0), cblk(heads), tab, tab,
                  pl.BlockSpec((hd, hd), lambda hi, bi: (0, 0)), chunk],
        out_specs=[pl.BlockSpec((None, seq, hd), lambda hi, bi: (bi, 0, hi)), chunk],
        out_shape=[jax.ShapeDtypeStruct((b, seq, heads * hd), BF16), jax.ShapeDtypeStruct(cast.shape, BF16)],
        scratch_shapes=[pltpu.VMEM((seq, hd), BF16), pltpu.VMEM((seq, hd), BF16),
                        pltpu.VMEM((seq, hd), BF16), pltpu.VMEM((seq, hd), BF16),
                        pltpu.VMEM((seq // c, hd, hd), BF16),
                        pltpu.VMEM((hd, hd), F32), pltpu.VMEM((hd, hd), F32),
                        pltpu.VMEM((c, c), F32),
                        pltpu.VMEM((c, hd), F32), pltpu.VMEM((c, hd), F32),
                        pltpu.VMEM((c, hd), F32), pltpu.VMEM((c, hd), F32)],
        compiler_params=_cparams(("arbitrary", "arbitrary")),
        name="retention",
    )(log_gammas, proj3, proj3, proj3, proj3, kvc3, kvc3, cos_t, sin_t, _swap_matrix(), cast)


def _pool_kernel(u_ref, bc_ref, cnt_ref, gw_ref, ps_ref, o_ref):
    seq = u_ref.shape[0]
    pg = gw_ref.shape[1]
    blk = bc_ref.shape[1]
    n_slab = seq // GRID_W
    for gi, w in enumerate(POOL_WINDOWS):
        lo = w // 2
        hi = w - lo - 1
        ub = u_ref[:, gi * pg:(gi + 1) * pg]
        s1 = jnp.concatenate(
            [jnp.dot(bc_ref[gi], ub[r * blk:(r + 1) * blk], preferred_element_type=F32)
             for r in range(seq // blk)], axis=0)
        zpad = lambda n: jnp.zeros((n * GRID_W, pg), F32)
        a = jnp.concatenate([zpad(lo), s1, zpad(hi)], axis=0) if hi else jnp.concatenate([zpad(lo), s1], axis=0)
        step = 1
        while step < w:
            n_rows = a.shape[0] - step * GRID_W
            a = a[:n_rows] + a[step * GRID_W:]
            step *= 2
        assert a.shape[0] == n_slab * GRID_W
        cnt = cnt_ref[gi]
        m = a / (cnt if pg == 128 else jnp.concatenate([cnt] * (pg // 128), axis=1))
        d = (m - ub.astype(F32)).astype(BF16)
        y = jnp.dot(d, gw_ref[gi], preferred_element_type=F32) * ps_ref[:, gi * pg:(gi + 1) * pg]
        o_ref[:, gi * pg:(gi + 1) * pg] = y.astype(o_ref.dtype)


def _pool_tables(seq, blk):
    rows = seq // GRID_W
    bands = np.zeros((len(POOL_WINDOWS), blk, blk), np.float32)
    cnts = np.zeros((len(POOL_WINDOWS), seq, 128), np.float32)
    t = np.arange(seq)
    r, c = t // GRID_W, t % GRID_W
    for gi, w in enumerate(POOL_WINDOWS):
        lo = w // 2
        hi = w - lo - 1
        col = np.arange(GRID_W)
        band = ((col[None, :] >= col[:, None] - lo) & (col[None, :] <= col[:, None] + hi)).astype(np.float32)
        bands[gi] = np.kron(np.eye(blk // GRID_W, dtype=np.float32), band)
        cc = np.minimum(c + hi + 1, GRID_W) - np.maximum(c - lo, 0)
        cr = np.minimum(r + hi + 1, rows) - np.maximum(r - lo, 0)
        cnts[gi] = (cc * cr).astype(np.float32)[:, None]
    return jnp.asarray(bands, BF16), jnp.asarray(cnts, F32)


def _pool(proj3, col_block, pool_w, pool_scale):
    b, seq, _ = proj3.shape
    n_g, pg, _ = pool_w.shape
    pw = n_g * pg
    blk = min(256, seq)
    bands, cnts = _pool_tables(seq, blk)
    return pl.pallas_call(
        _pool_kernel,
        grid=(b,),
        in_specs=[pl.BlockSpec((None, seq, pw), lambda bi: (bi, 0, col_block)),
                  pl.BlockSpec((n_g, blk, blk), lambda bi: (0, 0, 0)),
                  pl.BlockSpec((n_g, seq, 128), lambda bi: (0, 0, 0)),
                  pl.BlockSpec((n_g, pg, pg), lambda bi: (0, 0, 0)),
                  pl.BlockSpec((1, pw), lambda bi: (0, 0))],
        out_specs=pl.BlockSpec((None, seq, pw), lambda bi: (bi, 0, 0)),
        out_shape=jax.ShapeDtypeStruct((b, seq, pw), BF16),
        compiler_params=_cparams(("parallel",)),
        name="pool",
    )(proj3, bands, cnts, pool_w, pool_scale)


def _merge_kernel(yr_ref, yp_ref, gr_ref, gp_ref, wr_ref, wp_ref, o_ref):
    a = jnp.dot(yr_ref[...], wr_ref[...], preferred_element_type=F32)
    p = jnp.dot(yp_ref[...], wp_ref[...], preferred_element_type=F32)
    o_ref[...] = (_sigmoid(gr_ref[...].astype(F32)) * a + _sigmoid(gp_ref[...].astype(F32)) * p).astype(o_ref.dtype)


def _merge(y_ret, y_pool, proj, gate_col0, w_ret_out, w_pool_out):
    t, d = y_ret.shape
    tm = min(512, t)
    tn = min(1024, d)
    nj = d // tn
    return pl.pallas_call(
        _merge_kernel,
        grid=(t // tm, nj),
        in_specs=[pl.BlockSpec((tm, d), lambda i, j: (i, 0)),
                  pl.BlockSpec((tm, d), lambda i, j: (i, 0)),
                  pl.BlockSpec((tm, tn), lambda i, j: (i, gate_col0 * nj + j)),
                  pl.BlockSpec((tm, tn), lambda i, j: (i, (gate_col0 + 1) * nj + j)),
                  pl.BlockSpec((d, tn), lambda i, j: (0, j)),
                  pl.BlockSpec((d, tn), lambda i, j: (0, j))],
        out_specs=pl.BlockSpec((tm, tn), lambda i, j: (i, j)),
        out_shape=jax.ShapeDtypeStruct((t, d), BF16),
        compiler_params=_cparams(("parallel", "parallel")),
        name="merge",
    )(y_ret, y_pool, proj, proj, w_ret_out, w_pool_out)


NEG_BIG = -3.0e38
LANE_NONE = 1.0e9


def _route_block(logits):
    lane = lax.broadcasted_iota(jnp.int32, logits.shape, 1).astype(F32)

    def first_max(v):
        m = jnp.max(v, axis=1, keepdims=True)
        return m, jnp.min(jnp.where(v == m, lane, LANE_NONE), axis=1, keepdims=True)

    is_g = lane < N_GROUPS
    gmax, g_idx = first_max(jnp.where(is_g, logits, NEG_BIG))
    gden = jnp.sum(jnp.where(is_g, jnp.exp(logits - gmax), 0.0), axis=1, keepdims=True)
    lo = N_GROUPS + EXPERTS_PER_GROUP * g_idx
    in_grp = (lane >= lo) & (lane < lo + EXPERTS_PER_GROUP)
    le = jnp.where(in_grp, logits, NEG_BIG)
    m1, i1 = first_max(le)
    m2, i2 = first_max(jnp.where(lane == i1, NEG_BIG, le))
    eden = jnp.sum(jnp.where(in_grp, jnp.exp(logits - m1), 0.0), axis=1, keepdims=True)
    pg = 1.0 / gden
    w1 = pg * (1.0 / eden)
    w2 = pg * (jnp.exp(m2 - m1) / eden)
    return w1, w2, i1 - N_GROUPS, i2 - N_GROUPS


def _post_kernel(m_ref, x_ref, wo_ref, gpm_ref, gpf_ref, gt_ref, sh_ref, sc_ref, wr_ref, br_ref, tri_ref,
                 x1_ref, h2_ref, rt_ref, cnt_ref, carry):
    @pl.when(pl.program_id(0) == 0)
    def _():
        carry[...] = jnp.zeros(carry.shape, F32)

    mix = jnp.dot(m_ref[...], wo_ref[...], preferred_element_type=F32)
    x1 = x_ref[...] + gt_ref[0] * _rms(mix, gpm_ref[...])
    x1_ref[...] = x1
    h2 = _rms(x1, gpf_ref[...]) * (1.0 + sc_ref[0]) + sh_ref[0]
    tm = h2.shape[0]
    nl = h2.shape[1] // LANE_TILE
    for j in range(nl):
        h2_ref[pl.ds(j, tm, stride=nl), :] = h2[:, j * LANE_TILE:(j + 1) * LANE_TILE]
    logits = jnp.dot(h2.astype(BF16), wr_ref[...], preferred_element_type=F32) + br_ref[...]
    w1, w2, e1, e2 = _route_block(logits)

    lane = lax.broadcasted_iota(jnp.int32, logits.shape, 1).astype(F32)
    oh1 = lane == e1
    oh2 = lane == e2
    both = jnp.where(oh1 | oh2, 1.0, 0.0)
    base = jnp.dot(tri_ref[...], both.astype(BF16), preferred_element_type=F32) + carry[0:1, :]
    r1 = jnp.sum(jnp.where(oh1, base, 0.0), axis=1, keepdims=True)
    r2 = jnp.sum(jnp.where(oh2, base, 0.0), axis=1, keepdims=True)
    new_carry = carry[...] + jnp.sum(both, axis=0, keepdims=True)
    carry[...] = new_carry
    cnt_ref[...] = new_carry

    cols = (w1, w2, e1, e2, r1, r2)
    out = jnp.zeros(logits.shape, F32)
    for k, col in enumerate(cols):
        out = jnp.where(lane == float(k), col, out)
    rt_ref[...] = out


def _post(merged, x2, w_o, g_post_mix, g_pre_ffn, gt_m, sh_f, sc_f, w_router, b_router, seq):
    t, d = x2.shape
    tm = min(256, seq)
    per_b = seq // tm
    nl = d // LANE_TILE
    vec = pl.BlockSpec((1, d), lambda i: (0, 0))
    bvec = pl.BlockSpec((1, 1, d), lambda i: (i // per_b, 0, 0))
    row = pl.BlockSpec((tm, d), lambda i: (i, 0))
    tri = jnp.asarray(np.tril(np.ones((tm, tm), np.float32), -1), BF16)
    return pl.pallas_call(
        _post_kernel,
        grid=(t // tm,),
        in_specs=[row, row,
                  pl.BlockSpec((d, d), lambda i: (0, 0)),
                  vec, vec, bvec, bvec, bvec,
                  pl.BlockSpec((d, ROUTER_LANES), lambda i: (0, 0)),
                  pl.BlockSpec((1, ROUTER_LANES), lambda i: (0, 0)),
                  pl.BlockSpec((tm, tm), lambda i: (0, 0))],
        out_specs=[row, pl.BlockSpec((tm * nl, LANE_TILE), lambda i: (i, 0)),
                   pl.BlockSpec((tm, ROUTER_LANES), lambda i: (i, 0)),
                   pl.BlockSpec((8, ROUTER_LANES), lambda i: (0, 0))],
        out_shape=[jax.ShapeDtypeStruct((t, d), F32), jax.ShapeDtypeStruct((t * nl, LANE_TILE), F32),
                   jax.ShapeDtypeStruct((t, ROUTER_LANES), F32),
                   jax.ShapeDtypeStruct((8, ROUTER_LANES), F32)],
        scratch_shapes=[pltpu.VMEM((8, ROUTER_LANES), F32)],
        compiler_params=_cparams(("arbitrary",)),
        name="post",
    )(merged, x2, w_o, g_post_mix, g_pre_ffn, gt_m, sh_f, sc_f, w_router, b_router, tri)


def _expert_kernel(be_ref, nu_ref, rt_ref, h2_ref, wg_ref, wu_ref, wd_ref, o_ref, xbuf, sems):
    i = pl.program_id(0)
    n = pl.num_programs(0)
    nu = nu_ref[0]
    ns = xbuf.shape[0]
    bm = rt_ref.shape[1]
    nl = xbuf.shape[1] // bm

    def row_copy(step, r, s):
        src = pl.multiple_of(rt_ref[step, r] * nl, nl)
        dst = r * nl if isinstance(r, int) else pl.multiple_of(r * nl, nl)
        return pltpu.make_async_copy(h2_ref.at[pl.ds(src, nl)], xbuf.at[s, pl.ds(dst, nl)], sems.at[s])

    def wait_slot(s):
        pltpu.make_async_copy(h2_ref.at[pl.ds(0, bm * nl)], xbuf.at[s], sems.at[s]).wait()

    @pl.when(i == 0)
    def _():
        for blk in range(ns - 1):
            def body(r, carry):
                row_copy(blk, r, blk).start()
                return carry
            lax.fori_loop(0, bm, body, 0)

    @pl.when(i < nu)
    def _():
        slot = i % ns
        wait_slot(slot)
        view = xbuf.at[slot]
        x = jnp.concatenate([view[pl.ds(j, bm, stride=nl), :] for j in range(nl)], axis=1).astype(BF16)
        ahead = jnp.minimum(i + ns - 1, n - 1)
        s_ahead = (i + ns - 1) % ns
        for r in range(bm):
            row_copy(ahead, r, s_ahead).start()
        g = jnp.dot(x, wg_ref[...], preferred_element_type=F32)
        u = jnp.dot(x, wu_ref[...], preferred_element_type=F32)
        a = (_silu(g) * u).astype(BF16)
        y = jnp.dot(a, wd_ref[...], preferred_element_type=F32)
        for j in range(nl):
            o_ref[pl.ds(j, bm, stride=nl), :] = y[:, j * LANE_TILE:(j + 1) * LANE_TILE]

    @pl.when(i >= nu)
    def _():
        o_ref[...] = jnp.zeros(o_ref.shape, o_ref.dtype)

    @pl.when((i == nu) | ((i == n - 1) & (i < nu)))
    def _():
        last = jnp.where(i < nu, i + 1, i)
        for k in range(ns - 1):
            wait_slot((last + k) % ns)


def _experts(h2, row_tok2, block_e, n_used, w_gate, w_up, w_down):
    n_blocks, bm = row_tok2.shape
    assert n_blocks >= EXPERT_SLOTS
    d, ff = w_gate.shape[1:]
    nl = d // LANE_TILE
    return pl.pallas_call(
        _expert_kernel,
        grid_spec=pltpu.PrefetchScalarGridSpec(
            num_scalar_prefetch=3,
            grid=(n_blocks,),
            in_specs=[pl.BlockSpec(memory_space=pl.ANY),
                      pl.BlockSpec((None, d, ff), lambda i, be, nu, rt: (be[i], 0, 0)),
                      pl.BlockSpec((None, d, ff), lambda i, be, nu, rt: (be[i], 0, 0)),
                      pl.BlockSpec((None, ff, d), lambda i, be, nu, rt: (be[i], 0, 0))],
            out_specs=pl.BlockSpec((bm * nl, LANE_TILE), lambda i, be, nu, rt: (i, 0)),
            scratch_shapes=[pltpu.VMEM((EXPERT_SLOTS, bm * nl, LANE_TILE), F32),
                            pltpu.SemaphoreType.DMA((EXPERT_SLOTS,))]),
        out_shape=jax.ShapeDtypeStruct((n_blocks * bm * nl, LANE_TILE), F32),
        compiler_params=_cparams(("arbitrary",)),
        name="experts",
    )(block_e, n_used, row_tok2, h2, w_gate, w_up, w_down)


def _combine_kernel(pos_ref, wt_ref, x1_ref, g_ref, gt_ref, yb_ref, o_ref, buf, sems):
    i = pl.program_id(0)
    n = pl.num_programs(0)
    tm, d = x1_ref.shape
    nl = d // LANE_TILE
    ns = buf.shape[0]
    slot = i % ns

    def row_copy(step, r, kk, s):
        row = pl.multiple_of(pos_ref[step, TOP_K * r + kk] * nl, nl)
        dst = r * nl if isinstance(r, int) else pl.multiple_of(r * nl, nl)
        return pltpu.make_async_copy(yb_ref.at[pl.ds(row, nl)], buf.at[s, kk, pl.ds(dst, nl)], sems.at[s])

    def wait_slot(s):
        for kk in range(TOP_K):
            pltpu.make_async_copy(yb_ref.at[pl.ds(0, tm * nl)], buf.at[s, kk], sems.at[s]).wait()

    @pl.when(i == 0)
    def _():
        for blk in range(ns - 1):
            def body(r, carry):
                for kk in range(TOP_K):
                    row_copy(jnp.minimum(blk, n - 1), r, kk, blk).start()
                return carry
            lax.fori_loop(0, tm, body, 0)

    wait_slot(slot)
    ahead = jnp.minimum(i + ns - 1, n - 1)
    s_ahead = (i + ns - 1) % ns
    for r in range(tm):
        for kk in range(TOP_K):
            row_copy(ahead, r, kk, s_ahead).start()

    def rows(kk):
        view = buf.at[slot, kk]
        return jnp.concatenate([view[pl.ds(j, tm, stride=nl), :] for j in range(nl)], axis=1)

    wt = wt_ref[...]
    ffn = rows(0) * wt[:, 0:1] + rows(1) * wt[:, 1:2]
    o_ref[...] = x1_ref[...] + gt_ref[0] * _rms(ffn, g_ref[...])

    @pl.when(i == n - 1)
    def _():
        for k in range(1, ns):
            wait_slot((i + k) % ns)


def _combine(yb, pos2, wts, x1, g_post_ffn, gt_f, seq):
    t, d = x1.shape
    tm = pos2.shape[1] // TOP_K
    per_b = seq // tm
    return pl.pallas_call(
        _combine_kernel,
        grid_spec=pltpu.PrefetchScalarGridSpec(
            num_scalar_prefetch=1,
            grid=(t // tm,),
            in_specs=[pl.BlockSpec((tm, TOP_K), lambda i, p: (i, 0)),
                      pl.BlockSpec((tm, d), lambda i, p: (i, 0)),
                      pl.BlockSpec((1, d), lambda i, p: (0, 0)),
                      pl.BlockSpec((1, 1, d), lambda i, p: (i // per_b, 0, 0)),
                      pl.BlockSpec(memory_space=pl.ANY)],
            out_specs=pl.BlockSpec((tm, d), lambda i, p: (i, 0)),
            scratch_shapes=[pltpu.VMEM((COMBINE_SLOTS, TOP_K, tm * (d // LANE_TILE), LANE_TILE), F32),
                            pltpu.SemaphoreType.DMA((COMBINE_SLOTS,))]),
        out_shape=jax.ShapeDtypeStruct((t, d), F32),
        compiler_params=_cparams(("arbitrary",)),
        name="combine",
    )(pos2, wts, x1, g_post_ffn, gt_f, yb)


def _rope_tables(seq):
    quarter = HEAD_DIM // 4
    freqs = ROPE_BASE ** (-jnp.arange(quarter, dtype=F32) / quarter)
    pos = jnp.arange(seq)
    rows = (pos // GRID_W).astype(F32)
    cols = (pos % GRID_W).astype(F32)
    ar = rows[:, None] * freqs[None, :]
    ac = cols[:, None] * freqs[None, :]
    cos_t = jnp.concatenate([jnp.cos(ar), jnp.cos(ar), jnp.cos(ac), jnp.cos(ac)], axis=1)
    sin_t = jnp.concatenate([-jnp.sin(ar), jnp.sin(ar), -jnp.sin(ac), jnp.sin(ac)], axis=1)
    return cos_t, sin_t


def _dispatch(route, counts, t):
    bm = MOE_ROWS
    n_assign = t * TOP_K
    e = route[:, 2:4].astype(jnp.int32)
    rank = route[:, 4:6].astype(jnp.int32)
    counts = counts.astype(jnp.int32)
    padded = ((counts + bm - 1) // bm) * bm
    pend = jnp.cumsum(padded)
    pstart = pend - padded
    onehot = e[:, :, None] == jnp.arange(N_EXPERTS, dtype=jnp.int32)[None, None, :]
    pos = (jnp.sum(jnp.where(onehot, pstart[None, None, :], 0), axis=-1) + rank).reshape(-1)
    n_blocks = -(-n_assign // bm) + N_EXPERTS
    tok_flat = jnp.repeat(jnp.arange(t, dtype=jnp.int32), TOP_K)
    row_tok = jnp.zeros((n_blocks * bm,), jnp.int32).at[pos].set(tok_flat)
    block_start = jnp.arange(n_blocks, dtype=jnp.int32) * bm
    block_e = jnp.minimum(jnp.sum(pend[None, :] <= block_start[:, None], axis=1), N_EXPERTS - 1)
    n_used = (pend[-1] // bm).astype(jnp.int32).reshape(1)
    return row_tok, block_e.astype(jnp.int32), pos, n_used


def kernel(x, c, ctx, c_ctx, w_mod, b_mod, g_pre_mix, g_post_mix, g_pre_ffn, g_post_ffn, w_in, ret_decay, pool_w, pool_scale, w_ret_out, w_pool_out, w_o, w_router_group, b_router_group, w_router_expert, b_router_expert, w_exp_gate, w_exp_up, w_exp_down):
    b, seq, d = x.shape
    lc = ctx.shape[1]
    heads = d // HEAD_DIM
    t = b * seq
    assert w_mod.shape[0] == 1, "single layer"
    x2 = x.reshape(t, d)

    pad = (-(b + 1)) % 8
    cc = jnp.concatenate([c, c_ctx[None, :], jnp.zeros((pad, d), F32)], axis=0)
    mod = _mod(cc, w_mod[0], b_mod[0][None, :])
    sh_m, sc_m, gt_m, sh_f, sc_f, gt_f = [mod[:b, k * d:(k + 1) * d].reshape(b, 1, d) for k in range(6)]
    csh_m = mod[b:b + 1, 0:d].reshape(1, 1, d)
    csc_m = mod[b:b + 1, d:2 * d].reshape(1, 1, d)
    log_gammas = -jnp.exp(ret_decay[0].astype(F32))

    w_in_b = w_in[0].astype(BF16)
    tn = min(1024, d)
    per_seg = d // tn
    n_e, _, ff = w_exp_gate.shape[1:]
    proj, (wg_b, wu_b) = _inproj(x2, g_pre_mix[0][None, :], sh_m, sc_m, w_in_b, seq, 0, 7 * per_seg, tn,
                                 cast=(w_exp_gate[0].reshape(n_e * d, ff), w_exp_up[0].reshape(n_e * d, ff)))
    kvc, _ = _inproj(ctx.reshape(b * lc, d), g_pre_mix[0][None, :], csh_m, csc_m, w_in_b, b * lc,
                     per_seg, 2 * per_seg, tn)
    proj3 = proj.reshape(b, seq, 7 * d)
    kvc3 = kvc.reshape(b, lc, 2 * d)

    cos_t, sin_t = _rope_tables(seq)
    y_ret, wd_b = _retention(proj3, kvc3, log_gammas, cos_t, sin_t, heads, w_exp_down[0].reshape(n_e * ff, d))
    y_ret = y_ret.reshape(t, d)
    y_pool = _pool(proj3, 4, pool_w[0].astype(BF16), pool_scale[0][None, :]).reshape(t, d)

    merged = _merge(y_ret, y_pool, proj, 5, w_ret_out[0].astype(BF16), w_pool_out[0].astype(BF16))

    n_r = N_GROUPS + N_EXPERTS
    w_router = jnp.concatenate([w_router_group[0], w_router_expert[0],
                                jnp.zeros((d, ROUTER_LANES - n_r), F32)], axis=1).astype(BF16)
    b_router = jnp.concatenate([b_router_group[0], b_router_expert[0],
                                jnp.zeros((ROUTER_LANES - n_r,), F32)])[None, :]
    x1, h2, route, counts = _post(merged, x2, w_o[0].astype(BF16), g_post_mix[0][None, :],
                                  g_pre_ffn[0][None, :], gt_m, sh_f, sc_f, w_router, b_router, seq)

    row_tok, block_e, pos, n_used = _dispatch(route, counts[0, :N_EXPERTS], t)
    yb = _experts(h2, row_tok.reshape(-1, MOE_ROWS), block_e, n_used, wg_b.reshape(n_e, d, ff),
                  wu_b.reshape(n_e, d, ff), wd_b.reshape(n_e, ff, d))
    tm_c = min(256, seq)
    out = _combine(yb, pos.reshape(-1, TOP_K * tm_c), route[:, 0:TOP_K], x1, g_post_ffn[0][None, :], gt_f, seq)
    return out.reshape(b, seq, d)
```

```python
import functools

import jax
import jax.numpy as jnp
import numpy as np
from jax import lax
from jax.experimental import pallas as pl
from jax.experimental.pallas import tpu as pltpu

F32 = jnp.float32
BF16 = jnp.bfloat16

HEAD_DIM = 256
GRID_W = 64
ROPE_BASE = 10000.0
POOL_WINDOWS = (2, 4, 8, 16)
N_GROUPS = 8
EXPERTS_PER_GROUP = 8
N_EXPERTS = N_GROUPS * EXPERTS_PER_GROUP
TOP_K = 2
EPS = 1e-6
KEY_SCALE = HEAD_DIM ** -0.5

RET_CHUNK = 256
MOE_ROWS = 256
COMBINE_SLOTS = 3
EXPERT_SLOTS = 3
LANE_TILE = 128
ROUTER_LANES = LANE_TILE
V7X_VMEM_LIMIT = 56 * 1024 * 1024


def _cparams(sem, vmem=V7X_VMEM_LIMIT):
    return pltpu.CompilerParams(dimension_semantics=sem, vmem_limit_bytes=vmem)


def _sigmoid(x):
    return 0.5 * jnp.tanh(0.5 * x) + 0.5


def _silu(x):
    return x * _sigmoid(x)


def _rms(x, g):
    return x * lax.rsqrt(jnp.mean(x * x, axis=-1, keepdims=True) + EPS) * g


def _mod_kernel(c_ref, w_ref, b_ref, o_ref):
    s = _silu(c_ref[...]).astype(BF16)
    o_ref[...] = jnp.dot(s, w_ref[...].astype(BF16), preferred_element_type=F32) + b_ref[...]


def _mod(cc, w_mod, b_mod):
    rows, d = cc.shape
    n = w_mod.shape[1]
    tn = min(1024, n)
    return pl.pallas_call(
        _mod_kernel,
        grid=(n // tn,),
        in_specs=[pl.BlockSpec((rows, d), lambda j: (0, 0)),
                  pl.BlockSpec((d, tn), lambda j: (0, j)),
                  pl.BlockSpec((1, tn), lambda j: (0, j))],
        out_specs=pl.BlockSpec((rows, tn), lambda j: (0, j)),
        out_shape=jax.ShapeDtypeStruct((rows, n), F32),
        compiler_params=_cparams(("parallel",)),
        name="mod",
    )(cc, w_mod, b_mod)


def _inproj_kernel(n_cast, x_ref, g_ref, sh_ref, sc_ref, w_ref, *rest):
    cast_in, o_ref, cast_out, h_scr = rest[:n_cast], rest[n_cast], rest[n_cast + 1:-1], rest[-1]

    @pl.when(pl.program_id(1) == 0)
    def _():
        y = _rms(x_ref[...], g_ref[...])
        h_scr[...] = (y * (1.0 + sc_ref[0]) + sh_ref[0]).astype(BF16)

    o_ref[...] = jnp.dot(h_scr[...], w_ref[...], preferred_element_type=F32).astype(o_ref.dtype)
    for src, dst in zip(cast_in, cast_out):
        dst[...] = src[...].astype(BF16)


def _inproj(x2, g, sh, sc, w, rows_per_mod, col_tile0, n_col_tiles, tn, cast=()):
    m, d = x2.shape
    tm = min(1024, rows_per_mod, m)
    ni = m // tm
    js = 1 << (n_col_tiles.bit_length() - 1)
    mod_of = (lambda i: (i * tm) // rows_per_mod) if sh.shape[0] > 1 else (lambda i: 0)
    chunk = lambda a: pl.BlockSpec((a.shape[0] // (ni * js), a.shape[1]),
                                   lambda i, j: (i * js + jnp.minimum(j, js - 1), 0))
    outs = pl.pallas_call(
        functools.partial(_inproj_kernel, len(cast)),
        grid=(ni, n_col_tiles),
        in_specs=[pl.BlockSpec((tm, d), lambda i, j: (i, 0)),
                  pl.BlockSpec((1, d), lambda i, j: (0, 0)),
                  pl.BlockSpec((1, 1, d), lambda i, j: (mod_of(i), 0, 0)),
                  pl.BlockSpec((1, 1, d), lambda i, j: (mod_of(i), 0, 0)),
                  pl.BlockSpec((d, tn), lambda i, j: (0, j + col_tile0))] + [chunk(a) for a in cast],
        out_specs=[pl.BlockSpec((tm, tn), lambda i, j: (i, j))] + [chunk(a) for a in cast],
        out_shape=[jax.ShapeDtypeStruct((m, n_col_tiles * tn), BF16)]
                  + [jax.ShapeDtypeStruct(a.shape, BF16) for a in cast],
        scratch_shapes=[pltpu.VMEM((tm, d), BF16)],
        compiler_params=_cparams(("parallel", "arbitrary")),
        name="inproj",
    )(x2, g, sh, sc, w, *cast)
    return outs[0], tuple(outs[1:])


def _ret_kernel(lg_ref, q_ref, k_ref, v_ref, g_ref, kc_ref, vc_ref, cos_ref, sin_ref, swp_ref, wsrc_ref,
                o_ref, wdst_ref, qs, ks, kdf, kdb, sbs, sf, sb, dmat, dqf, dkf, dqb, dkb):
    h = pl.program_id(0)
    seq = q_ref.shape[0]
    lc = kc_ref.shape[0]
    c = RET_CHUNK
    n_chunks = seq // c
    lgf = lg_ref[0, h]
    lgb = lg_ref[1, h]

    wdst_ref[...] = wsrc_ref[...].astype(BF16)

    @pl.when(pl.program_id(1) == 0)
    def _():
        ri = lax.broadcasted_iota(jnp.int32, (c, c), 0)
        ci = lax.broadcasted_iota(jnp.int32, (c, c), 1)
        diff = (ri - ci).astype(F32)
        dmat[...] = (jnp.where(diff >= 0, jnp.exp(lgf * jnp.maximum(diff, 0.0)), 0.0)
                     + jnp.where(diff <= 0, jnp.exp(lgb * jnp.maximum(-diff, 0.0)), 0.0))
        rows = lax.broadcasted_iota(jnp.int32, (c, HEAD_DIM), 0).astype(F32)
        dqf[...] = jnp.exp(lgf * (rows + 1.0))
        dkf[...] = jnp.exp(lgf * (c - 1.0 - rows))
        dqb[...] = jnp.exp(lgb * (c - rows))
        dkb[...] = jnp.exp(lgb * rows)

    gf_c = dqf[c - 1:c, :]
    gb_c = dqb[0:1, :]

    def rope(t_ref):
        tb = t_ref[...]
        sw = jnp.dot(tb, swp_ref[...], preferred_element_type=F32)
        return tb.astype(F32) * cos_ref[...] + sw * sin_ref[...]

    qs[...] = rope(q_ref).astype(BF16)
    kr = (rope(k_ref) * KEY_SCALE).reshape(n_chunks, c, HEAD_DIM)
    ks[...] = kr.reshape(seq, HEAD_DIM).astype(BF16)
    kdf[...] = (kr * dkf[...][None]).reshape(seq, HEAD_DIM).astype(BF16)
    kdb[...] = (kr * dkb[...][None]).reshape(seq, HEAD_DIM).astype(BF16)

    pos = lax.broadcasted_iota(jnp.int32, (lc, HEAD_DIM), 0).astype(F32)
    kc = kc_ref[...].astype(F32) * KEY_SCALE
    vc = vc_ref[...]
    tdot = functools.partial(lax.dot_general, dimension_numbers=(((0,), (0,)), ((), ())),
                             preferred_element_type=F32)
    sf[...] = tdot((kc * jnp.exp(lgf * (lc - 1.0 - pos))).astype(BF16), vc)
    sb[...] = tdot((kc * jnp.exp(lgb * pos)).astype(BF16), vc)

    for i in reversed(range(n_chunks)):
        r = slice(i * c, (i + 1) * c)
        st = sb[...]
        sbs[i] = st.astype(BF16)
        if i > 0:
            sb[...] = st * gb_c + tdot(kdb[r, :], v_ref[r, :])

    for i in range(n_chunks):
        r = slice(i * c, (i + 1) * c)
        qc = qs[r, :]
        vv = v_ref[r, :]
        s = lax.dot_general(qc, ks[r, :], (((1,), (1,)), ((), ())), preferred_element_type=F32)
        st = sf[...]
        o = jnp.dot((s * dmat[...]).astype(BF16), vv, preferred_element_type=F32)
        o = o + jnp.dot(qc, st.astype(BF16), preferred_element_type=F32) * dqf[...]
        o = o + jnp.dot(qc, sbs[i], preferred_element_type=F32) * dqb[...]
        if i + 1 < n_chunks:
            sf[...] = st * gf_c + tdot(kdf[r, :], vv)
        o = o * lax.rsqrt(jnp.mean(o * o, axis=-1, keepdims=True) + EPS)
        o_ref[r, :] = (_silu(g_ref[r, :].astype(F32)) * o).astype(o_ref.dtype)


def _swap_matrix():
    half = HEAD_DIM // 2
    lane = np.arange(HEAD_DIM)
    partner = np.where(lane % half < half // 2, lane + half // 2, lane - half // 2)
    p = np.zeros((HEAD_DIM, HEAD_DIM), np.float32)
    p[partner, lane] = 1.0
    return jnp.asarray(p, BF16)


def _retention(proj3, kvc3, log_gammas, cos_t, sin_t, heads, cast):
    b, seq, _ = proj3.shape
    lc = kvc3.shape[1]
    hd = HEAD_DIM
    blk = lambda off: pl.BlockSpec((None, seq, hd), lambda hi, bi: (bi, 0, off + hi))
    cblk = lambda off: pl.BlockSpec((None, lc, hd), lambda hi, bi: (bi, 0, off + hi))
    tab = pl.BlockSpec((seq, hd), lambda hi, bi: (0, 0))
    chunk = pl.BlockSpec((cast.shape[0] // (heads * b), cast.shape[1]), lambda hi, bi: (hi * b + bi, 0))
    c = RET_CHUNK
    return pl.pallas_call(
        _ret_kernel,
        grid=(heads, b),
        in_specs=[pl.BlockSpec(memory_space=pltpu.SMEM),
                  blk(0), blk(heads), blk(2 * heads), blk(3 * heads),
                  cblk(0), cblk(heads), tab, tab,
                  pl.BlockSpec((hd, hd), lambda hi, bi: (0, 0)), chunk],
        out_specs=[pl.BlockSpec((None, seq, hd), lambda hi, bi: (bi, 0, hi)), chunk],
        out_shape=[jax.ShapeDtypeStruct((b, seq, heads * hd), BF16), jax.ShapeDtypeStruct(cast.shape, BF16)],
        scratch_shapes=[pltpu.VMEM((seq, hd), BF16), pltpu.VMEM((seq, hd), BF16),
                        pltpu.VMEM((seq, hd), BF16), pltpu.VMEM((seq, hd), BF16),
                        pltpu.VMEM((seq // c, hd, hd), BF16),
                        pltpu.VMEM((hd, hd), F32), pltpu.VMEM((hd, hd), F32),
                        pltpu.VMEM((c, c), F32),
                        pltpu.VMEM((c, hd), F32), pltpu.VMEM((c, hd), F32),
                        pltpu.VMEM((c, hd), F32), pltpu.VMEM((c, hd), F32)],
        compiler_params=_cparams(("arbitrary", "arbitrary")),
        name="retention",
    )(log_gammas, proj3, proj3, proj3, proj3, kvc3, kvc3, cos_t, sin_t, _swap_matrix(), cast)


def _pool_kernel(u_ref, bc_ref, cnt_ref, gw_ref, ps_ref, o_ref):
    seq = u_ref.shape[0]
    pg = gw_ref.shape[1]
    blk = bc_ref.shape[1]
    n_slab = seq // GRID_W
    for gi, w in enumerate(POOL_WINDOWS):
        lo = w // 2
        hi = w - lo - 1
        ub = u_ref[:, gi * pg:(gi + 1) * pg]
        s1 = jnp.concatenate(
            [jnp.dot(bc_ref[gi], ub[r * blk:(r + 1) * blk], preferred_element_type=F32)
             for r in range(seq // blk)], axis=0)
        zpad = lambda n: jnp.zeros((n * GRID_W, pg), F32)
        a = jnp.concatenate([zpad(lo), s1, zpad(hi)], axis=0) if hi else jnp.concatenate([zpad(lo), s1], axis=0)
        step = 1
        while step < w:
            n_rows = a.shape[0] - step * GRID_W
            a = a[:n_rows] + a[step * GRID_W:]
            step *= 2
        assert a.shape[0] == n_slab * GRID_W
        cnt = cnt_ref[gi]
        m = a / (cnt if pg == 128 else jnp.concatenate([cnt] * (pg // 128), axis=1))
        d = (m - ub.astype(F32)).astype(BF16)
        y = jnp.dot(d, gw_ref[gi], preferred_element_type=F32) * ps_ref[:, gi * pg:(gi + 1) * pg]
        o_ref[:, gi * pg:(gi + 1) * pg] = y.astype(o_ref.dtype)


def _pool_tables(seq, blk):
    rows = seq // GRID_W
    bands = np.zeros((len(POOL_WINDOWS), blk, blk), np.float32)
    cnts = np.zeros((len(POOL_WINDOWS), seq, 128), np.float32)
    t = np.arange(seq)
    r, c = t // GRID_W, t % GRID_W
    for gi, w in enumerate(POOL_WINDOWS):
        lo = w // 2
        hi = w - lo - 1
        col = np.arange(GRID_W)
        band = ((col[None, :] >= col[:, None] - lo) & (col[None, :] <= col[:, None] + hi)).astype(np.float32)
        bands[gi] = np.kron(np.eye(blk // GRID_W, dtype=np.float32), band)
        cc = np.minimum(c + hi + 1, GRID_W) - np.maximum(c - lo, 0)
        cr = np.minimum(r + hi + 1, rows) - np.maximum(r - lo, 0)
        cnts[gi] = (cc * cr).astype(np.float32)[:, None]
    return jnp.asarray(bands, BF16), jnp.asarray(cnts, F32)


def _pool(proj3, col_block, pool_w, pool_scale):
    b, seq, _ = proj3.shape
    n_g, pg, _ = pool_w.shape
    pw = n_g * pg
    blk = min(256, seq)
    bands, cnts = _pool_tables(seq, blk)
    return pl.pallas_call(
        _pool_kernel,
        grid=(b,),
        in_specs=[pl.BlockSpec((None, seq, pw), lambda bi: (bi, 0, col_block)),
                  pl.BlockSpec((n_g, blk, blk), lambda bi: (0, 0, 0)),
                  pl.BlockSpec((n_g, seq, 128), lambda bi: (0, 0, 0)),
                  pl.BlockSpec((n_g, pg, pg), lambda bi: (0, 0, 0)),
                  pl.BlockSpec((1, pw), lambda bi: (0, 0))],
        out_specs=pl.BlockSpec((None, seq, pw), lambda bi: (bi, 0, 0)),
        out_shape=jax.ShapeDtypeStruct((b, seq, pw), BF16),
        compiler_params=_cparams(("parallel",)),
        name="pool",
    )(proj3, bands, cnts, pool_w, pool_scale)


def _merge_kernel(yr_ref, yp_ref, gr_ref, gp_ref, wr_ref, wp_ref, o_ref):
    a = jnp.dot(yr_ref[...], wr_ref[...], preferred_element_type=F32)
    p = jnp.dot(yp_ref[...], wp_ref[...], preferred_element_type=F32)
    o_ref[...] = (_sigmoid(gr_ref[...].astype(F32)) * a + _sigmoid(gp_ref[...].astype(F32)) * p).astype(o_ref.dtype)


def _merge(y_ret, y_pool, proj, gate_col0, w_ret_out, w_pool_out):
    t, d = y_ret.shape
    tm = min(512, t)
    tn = min(1024, d)
    nj = d // tn
    return pl.pallas_call(
        _merge_kernel,
        grid=(t // tm, nj),
        in_specs=[pl.BlockSpec((tm, d), lambda i, j: (i, 0)),
                  pl.BlockSpec((tm, d), lambda i, j: (i, 0)),
                  pl.BlockSpec((tm, tn), lambda i, j: (i, gate_col0 * nj + j)),
                  pl.BlockSpec((tm, tn), lambda i, j: (i, (gate_col0 + 1) * nj + j)),
                  pl.BlockSpec((d, tn), lambda i, j: (0, j)),
                  pl.BlockSpec((d, tn), lambda i, j: (0, j))],
        out_specs=pl.BlockSpec((tm, tn), lambda i, j: (i, j)),
        out_shape=jax.ShapeDtypeStruct((t, d), BF16),
        compiler_params=_cparams(("parallel", "parallel")),
        name="merge",
    )(y_ret, y_pool, proj, proj, w_ret_out, w_pool_out)


NEG_BIG = -3.0e38
LANE_NONE = 1.0e9


def _route_block(logits):
    lane = lax.broadcasted_iota(jnp.int32, logits.shape, 1).astype(F32)

    def first_max(v):
        m = jnp.max(v, axis=1, keepdims=True)
        return m, jnp.min(jnp.where(v == m, lane, LANE_NONE), axis=1, keepdims=True)

    is_g = lane < N_GROUPS
    gmax, g_idx = first_max(jnp.where(is_g, logits, NEG_BIG))
    gden = jnp.sum(jnp.where(is_g, jnp.exp(logits - gmax), 0.0), axis=1, keepdims=True)
    lo = N_GROUPS + EXPERTS_PER_GROUP * g_idx
    in_grp = (lane >= lo) & (lane < lo + EXPERTS_PER_GROUP)
    le = jnp.where(in_grp, logits, NEG_BIG)
    m1, i1 = first_max(le)
    m2, i2 = first_max(jnp.where(lane == i1, NEG_BIG, le))
    eden = jnp.sum(jnp.where(in_grp, jnp.exp(logits - m1), 0.0), axis=1, keepdims=True)
    pg = 1.0 / gden
    w1 = pg * (1.0 / eden)
    w2 = pg * (jnp.exp(m2 - m1) / eden)
    return w1, w2, i1 - N_GROUPS, i2 - N_GROUPS


def _post_kernel(m_ref, x_ref, wo_ref, gpm_ref, gpf_ref, gt_ref, sh_ref, sc_ref, wr_ref, br_ref, tri_ref,
                 x1_ref, h2_ref, rt_ref, cnt_ref, carry):
    @pl.when(pl.program_id(0) == 0)
    def _():
        carry[...] = jnp.zeros(carry.shape, F32)

    tm, d = x_ref.shape
    sub = tri_ref.shape[0]
    nl = d // LANE_TILE
    counts = carry[...]
    for sb in range(tm // sub):
        r = slice(sb * sub, (sb + 1) * sub)
        mix = jnp.dot(m_ref[r, :], wo_ref[...], preferred_element_type=F32)
        x1 = x_ref[r, :] + gt_ref[0] * _rms(mix, gpm_ref[...])
        x1_ref[r, :] = x1
        h2 = _rms(x1, gpf_ref[...]) * (1.0 + sc_ref[0]) + sh_ref[0]
        for j in range(nl):
            h2_ref[pl.ds(sb * sub * nl + j, sub, stride=nl), :] = h2[:, j * LANE_TILE:(j + 1) * LANE_TILE]
        logits = jnp.dot(h2.astype(BF16), wr_ref[...], preferred_element_type=F32) + br_ref[...]
        w1, w2, e1, e2 = _route_block(logits)

        lane = lax.broadcasted_iota(jnp.int32, logits.shape, 1).astype(F32)
        oh1 = lane == e1
        oh2 = lane == e2
        both = jnp.where(oh1 | oh2, 1.0, 0.0)
        base = jnp.dot(tri_ref[...], both.astype(BF16), preferred_element_type=F32) + counts[0:1, :]
        r1 = jnp.sum(jnp.where(oh1, base, 0.0), axis=1, keepdims=True)
        r2 = jnp.sum(jnp.where(oh2, base, 0.0), axis=1, keepdims=True)
        counts = counts + jnp.sum(both, axis=0, keepdims=True)

        out = jnp.zeros(logits.shape, F32)
        for k, col in enumerate((w1, w2, e1, e2, r1, r2)):
            out = jnp.where(lane == float(k), col, out)
        rt_ref[r, :] = out
    carry[...] = counts
    cnt_ref[...] = counts


def _post(merged, x2, w_o, g_post_mix, g_pre_ffn, gt_m, sh_f, sc_f, w_router, b_router, seq):
    t, d = x2.shape
    tm = min(512, seq)
    sub = min(256, tm)
    per_b = seq // tm
    nl = d // LANE_TILE
    vec = pl.BlockSpec((1, d), lambda i: (0, 0))
    bvec = pl.BlockSpec((1, 1, d), lambda i: (i // per_b, 0, 0))
    row = pl.BlockSpec((tm, d), lambda i: (i, 0))
    tri = jnp.asarray(np.tril(np.ones((sub, sub), np.float32), -1), BF16)
    return pl.pallas_call(
        _post_kernel,
        grid=(t // tm,),
        in_specs=[row, row,
                  pl.BlockSpec((d, d), lambda i: (0, 0), pipeline_mode=pl.Buffered(1)),
                  vec, vec, bvec, bvec, bvec,
                  pl.BlockSpec((d, ROUTER_LANES), lambda i: (0, 0)),
                  pl.BlockSpec((1, ROUTER_LANES), lambda i: (0, 0)),
                  pl.BlockSpec((sub, sub), lambda i: (0, 0))],
        out_specs=[row, pl.BlockSpec((tm * nl, LANE_TILE), lambda i: (i, 0)),
                   pl.BlockSpec((tm, ROUTER_LANES), lambda i: (i, 0)),
                   pl.BlockSpec((8, ROUTER_LANES), lambda i: (0, 0))],
        out_shape=[jax.ShapeDtypeStruct((t, d), F32), jax.ShapeDtypeStruct((t * nl, LANE_TILE), F32),
                   jax.ShapeDtypeStruct((t, ROUTER_LANES), F32),
                   jax.ShapeDtypeStruct((8, ROUTER_LANES), F32)],
        scratch_shapes=[pltpu.VMEM((8, ROUTER_LANES), F32)],
        compiler_params=_cparams(("arbitrary",)),
        name="post",
    )(merged, x2, w_o, g_post_mix, g_pre_ffn, gt_m, sh_f, sc_f, w_router, b_router, tri)


def _expert_kernel(be_ref, nu_ref, rt_ref, first_ref, par_ref, nxt_ref, h2_ref, wg_ref, wu_ref, wd_ref, o_ref,
                   xbuf, wgb, wub, wdb, sems, wsems):
    i = pl.program_id(0)
    n = pl.num_programs(0)
    nu = nu_ref[0]
    ns = xbuf.shape[0]
    bm = rt_ref.shape[1]
    nl = xbuf.shape[1] // bm
    n_experts = wg_ref.shape[0]

    def weight_copies(e, s):
        return [pltpu.make_async_copy(src.at[e], dst.at[s], wsems.at[s])
                for src, dst in ((wg_ref, wgb), (wu_ref, wub), (wd_ref, wdb))]

    @pl.when(i == 0)
    def _():
        for cp in weight_copies(be_ref[0], 0):
            cp.start()

    @pl.when((i < nu) & (first_ref[i] == 1))
    def _():
        wslot = par_ref[i]
        for cp in weight_copies(be_ref[i], wslot):
            cp.wait()

        @pl.when(nxt_ref[i] < n_experts)
        def _():
            for cp in weight_copies(nxt_ref[i], 1 - wslot):
                cp.start()

    def row_copy(step, r, s):
        src = pl.multiple_of(rt_ref[step, r] * nl, nl)
        dst = r * nl if isinstance(r, int) else pl.multiple_of(r * nl, nl)
        return pltpu.make_async_copy(h2_ref.at[pl.ds(src, nl)], xbuf.at[s, pl.ds(dst, nl)], sems.at[s])

    def wait_slot(s):
        pltpu.make_async_copy(h2_ref.at[pl.ds(0, bm * nl)], xbuf.at[s], sems.at[s]).wait()

    @pl.when(i == 0)
    def _():
        for blk in range(ns - 1):
            def body(r, carry):
                row_copy(blk, r, blk).start()
                return carry
            lax.fori_loop(0, bm, body, 0)

    @pl.when(i < nu)
    def _():
        slot = i % ns
        wait_slot(slot)
        view = xbuf.at[slot]
        x = jnp.concatenate([view[pl.ds(j, bm, stride=nl), :] for j in range(nl)], axis=1).astype(BF16)
        ahead = jnp.minimum(i + ns - 1, n - 1)
        s_ahead = (i + ns - 1) % ns
        for r in range(bm):
            row_copy(ahead, r, s_ahead).start()
        wslot = par_ref[i]
        g = jnp.dot(x, wgb[wslot], preferred_element_type=F32)
        u = jnp.dot(x, wub[wslot], preferred_element_type=F32)
        a = (_silu(g) * u).astype(BF16)
        y = jnp.dot(a, wdb[wslot], preferred_element_type=F32)
        for j in range(nl):
            o_ref[pl.ds(j, bm, stride=nl), :] = y[:, j * LANE_TILE:(j + 1) * LANE_TILE]

    @pl.when(i >= nu)
    def _():
        o_ref[...] = jnp.zeros(o_ref.shape, o_ref.dtype)

    @pl.when((i == nu) | ((i == n - 1) & (i < nu)))
    def _():
        last = jnp.where(i < nu, i + 1, i)
        for k in range(ns - 1):
            wait_slot((last + k) % ns)


def _experts(h2, row_tok2, block_e, n_used, run_first, run_parity, next_e, w_gate, w_up, w_down):
    n_blocks, bm = row_tok2.shape
    assert n_blocks >= EXPERT_SLOTS
    d, ff = w_gate.shape[1:]
    nl = d // LANE_TILE
    hbm = pl.BlockSpec(memory_space=pl.ANY)
    return pl.pallas_call(
        _expert_kernel,
        grid_spec=pltpu.PrefetchScalarGridSpec(
            num_scalar_prefetch=6,
            grid=(n_blocks,),
            in_specs=[hbm, hbm, hbm, hbm],
            out_specs=pl.BlockSpec((bm * nl, LANE_TILE), lambda i, *_: (i, 0)),
            scratch_shapes=[pltpu.VMEM((EXPERT_SLOTS, bm * nl, LANE_TILE), F32),
                            pltpu.VMEM((2, d, ff), BF16), pltpu.VMEM((2, d, ff), BF16),
                            pltpu.VMEM((2, ff, d), BF16),
                            pltpu.SemaphoreType.DMA((EXPERT_SLOTS,)), pltpu.SemaphoreType.DMA((2,))]),
        out_shape=jax.ShapeDtypeStruct((n_blocks * bm * nl, LANE_TILE), F32),
        compiler_params=_cparams(("arbitrary",)),
        name="experts",
    )(block_e, n_used, row_tok2, run_first, run_parity, next_e, h2, w_gate, w_up, w_down)


def _combine_kernel(pos_ref, wt_ref, x1_ref, g_ref, gt_ref, yb_ref, o_ref, buf, sems):
    i = pl.program_id(0)
    n = pl.num_programs(0)
    tm, d = x1_ref.shape
    nl = d // LANE_TILE
    ns = buf.shape[0]
    slot = i % ns

    def row_copy(step, r, kk, s):
        row = pl.multiple_of(pos_ref[step, TOP_K * r + kk] * nl, nl)
        dst = r * nl if isinstance(r, int) else pl.multiple_of(r * nl, nl)
        return pltpu.make_async_copy(yb_ref.at[pl.ds(row, nl)], buf.at[s, kk, pl.ds(dst, nl)], sems.at[s])

    def wait_slot(s):
        for kk in range(TOP_K):
            pltpu.make_async_copy(yb_ref.at[pl.ds(0, tm * nl)], buf.at[s, kk], sems.at[s]).wait()

    @pl.when(i == 0)
    def _():
        for blk in range(ns - 1):
            def body(r, carry):
                for kk in range(TOP_K):
                    row_copy(jnp.minimum(blk, n - 1), r, kk, blk).start()
                return carry
            lax.fori_loop(0, tm, body, 0)

    wait_slot(slot)
    ahead = jnp.minimum(i + ns - 1, n - 1)
    s_ahead = (i + ns - 1) % ns
    for r in range(tm):
        for kk in range(TOP_K):
            row_copy(ahead, r, kk, s_ahead).start()

    def rows(kk):
        view = buf.at[slot, kk]
        return jnp.concatenate([view[pl.ds(j, tm, stride=nl), :] for j in range(nl)], axis=1)

    wt = wt_ref[...]
    ffn = rows(0) * wt[:, 0:1] + rows(1) * wt[:, 1:2]
    o_ref[...] = x1_ref[...] + gt_ref[0] * _rms(ffn, g_ref[...])

    @pl.when(i == n - 1)
    def _():
        for k in range(1, ns):
            wait_slot((i + k) % ns)


def _combine(yb, pos2, wts, x1, g_post_ffn, gt_f, seq):
    t, d = x1.shape
    tm = pos2.shape[1] // TOP_K
    per_b = seq // tm
    return pl.pallas_call(
        _combine_kernel,
        grid_spec=pltpu.PrefetchScalarGridSpec(
            num_scalar_prefetch=1,
            grid=(t // tm,),
            in_specs=[pl.BlockSpec((tm, TOP_K), lambda i, p: (i, 0)),
                      pl.BlockSpec((tm, d), lambda i, p: (i, 0)),
                      pl.BlockSpec((1, d), lambda i, p: (0, 0)),
                      pl.BlockSpec((1, 1, d), lambda i, p: (i // per_b, 0, 0)),
                      pl.BlockSpec(memory_space=pl.ANY)],
            out_specs=pl.BlockSpec((tm, d), lambda i, p: (i, 0)),
            scratch_shapes=[pltpu.VMEM((COMBINE_SLOTS, TOP_K, tm * (d // LANE_TILE), LANE_TILE), F32),
                            pltpu.SemaphoreType.DMA((COMBINE_SLOTS,))]),
        out_shape=jax.ShapeDtypeStruct((t, d), F32),
        compiler_params=_cparams(("arbitrary",)),
        name="combine",
    )(pos2, wts, x1, g_post_ffn, gt_f, yb)


def _rope_tables(seq):
    quarter = HEAD_DIM // 4
    freqs = ROPE_BASE ** (-jnp.arange(quarter, dtype=F32) / quarter)
    pos = jnp.arange(seq)
    rows = (pos // GRID_W).astype(F32)
    cols = (pos % GRID_W).astype(F32)
    ar = rows[:, None] * freqs[None, :]
    ac = cols[:, None] * freqs[None, :]
    cos_t = jnp.concatenate([jnp.cos(ar), jnp.cos(ar), jnp.cos(ac), jnp.cos(ac)], axis=1)
    sin_t = jnp.concatenate([-jnp.sin(ar), jnp.sin(ar), -jnp.sin(ac), jnp.sin(ac)], axis=1)
    return cos_t, sin_t


def _dispatch(route, counts, t):
    bm = MOE_ROWS
    n_assign = t * TOP_K
    e = route[:, 2:4].astype(jnp.int32)
    rank = route[:, 4:6].astype(jnp.int32)
    counts = counts.astype(jnp.int32)
    padded = ((counts + bm - 1) // bm) * bm
    pend = jnp.cumsum(padded)
    pstart = pend - padded
    onehot = e[:, :, None] == jnp.arange(N_EXPERTS, dtype=jnp.int32)[None, None, :]
    pos = (jnp.sum(jnp.where(onehot, pstart[None, None, :], 0), axis=-1) + rank).reshape(-1)
    n_blocks = -(-n_assign // bm) + N_EXPERTS
    tok_flat = jnp.repeat(jnp.arange(t, dtype=jnp.int32), TOP_K)
    row_tok = jnp.zeros((n_blocks * bm,), jnp.int32).at[pos].set(tok_flat, unique_indices=True)
    block_start = jnp.arange(n_blocks, dtype=jnp.int32) * bm
    block_e = jnp.minimum(jnp.sum(pend[None, :] <= block_start[:, None], axis=1), N_EXPERTS - 1).astype(jnp.int32)
    n_used = (pend[-1] // bm).astype(jnp.int32).reshape(1)
    active = jnp.arange(n_blocks, dtype=jnp.int32) < n_used[0]
    prev_e = jnp.concatenate([jnp.full((1,), -1, jnp.int32), block_e[:-1]])
    run_first = (active & (block_e != prev_e)).astype(jnp.int32)
    run_parity = jnp.maximum(jnp.cumsum(run_first) - 1, 0).astype(jnp.int32) % 2
    ids = jnp.arange(N_EXPERTS, dtype=jnp.int32)
    live = jnp.where(counts > 0, ids, N_EXPERTS)
    live_from = lax.cummin(live, reverse=True)
    next_live = jnp.concatenate([live_from[1:], jnp.full((1,), N_EXPERTS, jnp.int32)])
    next_e = jnp.sum(jnp.where(block_e[:, None] == ids[None, :], next_live[None, :], 0), axis=1).astype(jnp.int32)
    return row_tok, block_e, pos, n_used, run_first, run_parity, next_e


def kernel(x, c, ctx, c_ctx, w_mod, b_mod, g_pre_mix, g_post_mix, g_pre_ffn, g_post_ffn, w_in, ret_decay, pool_w, pool_scale, w_ret_out, w_pool_out, w_o, w_router_group, b_router_group, w_router_expert, b_router_expert, w_exp_gate, w_exp_up, w_exp_down):
    b, seq, d = x.shape
    lc = ctx.shape[1]
    heads = d // HEAD_DIM
    t = b * seq
    assert w_mod.shape[0] == 1, "single layer"
    x2 = x.reshape(t, d)

    pad = (-(b + 1)) % 8
    cc = jnp.concatenate([c, c_ctx[None, :], jnp.zeros((pad, d), F32)], axis=0)
    mod = _mod(cc, w_mod[0], b_mod[0][None, :])
    sh_m, sc_m, gt_m, sh_f, sc_f, gt_f = [mod[:b, k * d:(k + 1) * d].reshape(b, 1, d) for k in range(6)]
    csh_m = mod[b:b + 1, 0:d].reshape(1, 1, d)
    csc_m = mod[b:b + 1, d:2 * d].reshape(1, 1, d)
    log_gammas = -jnp.exp(ret_decay[0].astype(F32))

    w_in_b = w_in[0].astype(BF16)
    tn = min(1024, d)
    per_seg = d // tn
    n_e, _, ff = w_exp_gate.shape[1:]
    proj, (wg_b, wu_b) = _inproj(x2, g_pre_mix[0][None, :], sh_m, sc_m, w_in_b, seq, 0, 7 * per_seg, tn,
                                 cast=(w_exp_gate[0].reshape(n_e * d, ff), w_exp_up[0].reshape(n_e * d, ff)))
    kvc, _ = _inproj(ctx.reshape(b * lc, d), g_pre_mix[0][None, :], csh_m, csc_m, w_in_b, b * lc,
                     per_seg, 2 * per_seg, tn)
    proj3 = proj.reshape(b, seq, 7 * d)
    kvc3 = kvc.reshape(b, lc, 2 * d)

    cos_t, sin_t = _rope_tables(seq)
    y_ret, wd_b = _retention(proj3, kvc3, log_gammas, cos_t, sin_t, heads, w_exp_down[0].reshape(n_e * ff, d))
    y_ret = y_ret.reshape(t, d)
    y_pool = _pool(proj3, 4, pool_w[0].astype(BF16), pool_scale[0][None, :]).reshape(t, d)

    merged = _merge(y_ret, y_pool, proj, 5, w_ret_out[0].astype(BF16), w_pool_out[0].astype(BF16))

    n_r = N_GROUPS + N_EXPERTS
    w_router = jnp.concatenate([w_router_group[0], w_router_expert[0],
                                jnp.zeros((d, ROUTER_LANES - n_r), F32)], axis=1).astype(BF16)
    b_router = jnp.concatenate([b_router_group[0], b_router_expert[0],
                                jnp.zeros((ROUTER_LANES - n_r,), F32)])[None, :]
    x1, h2, route, counts = _post(merged, x2, w_o[0].astype(BF16), g_post_mix[0][None, :],
                                  g_pre_ffn[0][None, :], gt_m, sh_f, sc_f, w_router, b_router, seq)

    row_tok, block_e, pos, n_used, run_first, run_parity, next_e = _dispatch(route, counts[0, :N_EXPERTS], t)
    yb = _experts(h2, row_tok.reshape(-1, MOE_ROWS), block_e, n_used, run_first, run_parity, next_e,
                  wg_b.reshape(n_e, d, ff), wu_b.reshape(n_e, d, ff), wd_b.reshape(n_e, ff, d))
    tm_c = min(256, seq)
    out = _combine(yb, pos.reshape(-1, TOP_K * tm_c), route[:, 0:TOP_K], x1, g_post_ffn[0][None, :], gt_f, seq)
    return out.reshape(b, seq, d)
```

```python
import functools

import jax
import jax.numpy as jnp
import numpy as np
from jax import lax
from jax.experimental import pallas as pl
from jax.experimental.pallas import tpu as pltpu

F32 = jnp.float32
BF16 = jnp.bfloat16

HEAD_DIM = 256
GRID_W = 64
ROPE_BASE = 10000.0
POOL_WINDOWS = (2, 4, 8, 16)
N_GROUPS = 8
EXPERTS_PER_GROUP = 8
N_EXPERTS = N_GROUPS * EXPERTS_PER_GROUP
TOP_K = 2
EPS = 1e-6
KEY_SCALE = HEAD_DIM ** -0.5

RET_CHUNK = 256
MOE_ROWS = 256
COMBINE_SLOTS = 3
EXPERT_SLOTS = 3
LANE_TILE = 128
ROUTER_LANES = LANE_TILE
V7X_VMEM_LIMIT = 56 * 1024 * 1024


def _cparams(sem, vmem=V7X_VMEM_LIMIT):
    return pltpu.CompilerParams(dimension_semantics=sem, vmem_limit_bytes=vmem)


def _sigmoid(x):
    return 0.5 * jnp.tanh(0.5 * x) + 0.5


def _silu(x):
    return x * _sigmoid(x)


def _rms(x, g):
    return x * lax.rsqrt(jnp.mean(x * x, axis=-1, keepdims=True) + EPS) * g


def _mod_kernel(c_ref, w_ref, b_ref, o_ref):
    s = _silu(c_ref[...]).astype(BF16)
    o_ref[...] = jnp.dot(s, w_ref[...].astype(BF16), preferred_element_type=F32) + b_ref[...]


def _mod(cc, w_mod, b_mod):
    rows, d = cc.shape
    n = w_mod.shape[1]
    tn = min(1024, n)
    return pl.pallas_call(
        _mod_kernel,
        grid=(n // tn,),
        in_specs=[pl.BlockSpec((rows, d), lambda j: (0, 0)),
                  pl.BlockSpec((d, tn), lambda j: (0, j)),
                  pl.BlockSpec((1, tn), lambda j: (0, j))],
        out_specs=pl.BlockSpec((rows, tn), lambda j: (0, j)),
        out_shape=jax.ShapeDtypeStruct((rows, n), F32),
        compiler_params=_cparams(("parallel",)),
        name="mod",
    )(cc, w_mod, b_mod)


def _inproj_kernel(n_cast, x_ref, g_ref, sh_ref, sc_ref, w_ref, *rest):
    cast_in, o_ref, cast_out, h_scr = rest[:n_cast], rest[n_cast], rest[n_cast + 1:-1], rest[-1]

    @pl.when(pl.program_id(1) == 0)
    def _():
        y = _rms(x_ref[...], g_ref[...])
        h_scr[...] = (y * (1.0 + sc_ref[0]) + sh_ref[0]).astype(BF16)

    o_ref[...] = jnp.dot(h_scr[...], w_ref[...], preferred_element_type=F32).astype(o_ref.dtype)
    for src, dst in zip(cast_in, cast_out):
        dst[...] = src[...].astype(BF16)


def _inproj(x2, g, sh, sc, w, rows_per_mod, col_tile0, n_col_tiles, tn, cast=()):
    m, d = x2.shape
    tm = min(1024, rows_per_mod, m)
    ni = m // tm
    js = 1 << (n_col_tiles.bit_length() - 1)
    mod_of = (lambda i: (i * tm) // rows_per_mod) if sh.shape[0] > 1 else (lambda i: 0)
    chunk = lambda a: pl.BlockSpec((a.shape[0] // (ni * js), a.shape[1]),
                                   lambda i, j: (i * js + jnp.minimum(j, js - 1), 0))
    outs = pl.pallas_call(
        functools.partial(_inproj_kernel, len(cast)),
        grid=(ni, n_col_tiles),
        in_specs=[pl.BlockSpec((tm, d), lambda i, j: (i, 0)),
                  pl.BlockSpec((1, d), lambda i, j: (0, 0)),
                  pl.BlockSpec((1, 1, d), lambda i, j: (mod_of(i), 0, 0)),
                  pl.BlockSpec((1, 1, d), lambda i, j: (mod_of(i), 0, 0)),
                  pl.BlockSpec((d, tn), lambda i, j: (0, j + col_tile0))] + [chunk(a) for a in cast],
        out_specs=[pl.BlockSpec((tm, tn), lambda i, j: (i, j))] + [chunk(a) for a in cast],
        out_shape=[jax.ShapeDtypeStruct((m, n_col_tiles * tn), BF16)]
                  + [jax.ShapeDtypeStruct(a.shape, BF16) for a in cast],
        scratch_shapes=[pltpu.VMEM((tm, d), BF16)],
        compiler_params=_cparams(("parallel", "arbitrary")),
        name="inproj",
    )(x2, g, sh, sc, w, *cast)
    return outs[0], tuple(outs[1:])


def _ret_kernel(lg_ref, q_ref, k_ref, v_ref, g_ref, kc_ref, vc_ref, cos_ref, sin_ref, swp_ref, wsrc_ref,
                o_ref, wdst_ref, qs, ks, kdf, kdb, sbs, sf, sb, dmat, dqf, dkf, dqb, dkb):
    h = pl.program_id(0)
    seq = q_ref.shape[0]
    lc = kc_ref.shape[0]
    c = RET_CHUNK
    n_chunks = seq // c
    lgf = lg_ref[0, h]
    lgb = lg_ref[1, h]

    wdst_ref[...] = wsrc_ref[...].astype(BF16)

    @pl.when(pl.program_id(1) == 0)
    def _():
        ri = lax.broadcasted_iota(jnp.int32, (c, c), 0)
        ci = lax.broadcasted_iota(jnp.int32, (c, c), 1)
        diff = (ri - ci).astype(F32)
        dmat[...] = (jnp.where(diff >= 0, jnp.exp(lgf * jnp.maximum(diff, 0.0)), 0.0)
                     + jnp.where(diff <= 0, jnp.exp(lgb * jnp.maximum(-diff, 0.0)), 0.0))
        rows = lax.broadcasted_iota(jnp.int32, (c, HEAD_DIM), 0).astype(F32)
        dqf[...] = jnp.exp(lgf * (rows + 1.0))
        dkf[...] = jnp.exp(lgf * (c - 1.0 - rows))
        dqb[...] = jnp.exp(lgb * (c - rows))
        dkb[...] = jnp.exp(lgb * rows)

    gf_c = dqf[c - 1:c, :]
    gb_c = dqb[0:1, :]

    def rope(t_ref):
        tb = t_ref[...]
        sw = jnp.dot(tb, swp_ref[...], preferred_element_type=F32)
        return tb.astype(F32) * cos_ref[...] + sw * sin_ref[...]

    qs[...] = rope(q_ref).astype(BF16)
    kr = (rope(k_ref) * KEY_SCALE).reshape(n_chunks, c, HEAD_DIM)
    ks[...] = kr.reshape(seq, HEAD_DIM).astype(BF16)
    kdf[...] = (kr * dkf[...][None]).reshape(seq, HEAD_DIM).astype(BF16)
    kdb[...] = (kr * dkb[...][None]).reshape(seq, HEAD_DIM).astype(BF16)

    pos = lax.broadcasted_iota(jnp.int32, (lc, HEAD_DIM), 0).astype(F32)
    kc = kc_ref[...].astype(F32) * KEY_SCALE
    vc = vc_ref[...]
    tdot = functools.partial(lax.dot_general, dimension_numbers=(((0,), (0,)), ((), ())),
                             preferred_element_type=F32)
    sf[...] = tdot((kc * jnp.exp(lgf * (lc - 1.0 - pos))).astype(BF16), vc)
    sb[...] = tdot((kc * jnp.exp(lgb * pos)).astype(BF16), vc)

    for i in reversed(range(n_chunks)):
        r = slice(i * c, (i + 1) * c)
        st = sb[...]
        sbs[i] = st.astype(BF16)
        if i > 0:
            sb[...] = st * gb_c + tdot(kdb[r, :], v_ref[r, :])

    for i in range(n_chunks):
        r = slice(i * c, (i + 1) * c)
        qc = qs[r, :]
        vv = v_ref[r, :]
        s = lax.dot_general(qc, ks[r, :], (((1,), (1,)), ((), ())), preferred_element_type=F32)
        st = sf[...]
        o = jnp.dot((s * dmat[...]).astype(BF16), vv, preferred_element_type=F32)
        o = o + jnp.dot(qc, st.astype(BF16), preferred_element_type=F32) * dqf[...]
        o = o + jnp.dot(qc, sbs[i], preferred_element_type=F32) * dqb[...]
        if i + 1 < n_chunks:
            sf[...] = st * gf_c + tdot(kdf[r, :], vv)
        o = o * lax.rsqrt(jnp.mean(o * o, axis=-1, keepdims=True) + EPS)
        o_ref[r, :] = (_silu(g_ref[r, :].astype(F32)) * o).astype(o_ref.dtype)


def _swap_matrix():
    half = HEAD_DIM // 2
    lane = np.arange(HEAD_DIM)
    partner = np.where(lane % half < half // 2, lane + half // 2, lane - half // 2)
    p = np.zeros((HEAD_DIM, HEAD_DIM), np.float32)
    p[partner, lane] = 1.0
    return jnp.asarray(p, BF16)


def _retention(proj3, kvc3, log_gammas, cos_t, sin_t, heads, cast):
    b, seq, _ = proj3.shape
    lc = kvc3.shape[1]
    hd = HEAD_DIM
    blk = lambda off: pl.BlockSpec((None, seq, hd), lambda hi, bi: (bi, 0, off + hi))
    cblk = lambda off: pl.BlockSpec((None, lc, hd), lambda hi, bi: (bi, 0, off + hi))
    tab = pl.BlockSpec((seq, hd), lambda hi, bi: (0, 0))
    chunk = pl.BlockSpec((cast.shape[0] // (heads * b), cast.shape[1]), lambda hi, bi: (hi * b + bi, 0))
    c = RET_CHUNK
    return pl.pallas_call(
        _ret_kernel,
        grid=(heads, b),
        in_specs=[pl.BlockSpec(memory_space=pltpu.SMEM),
                  blk(0), blk(heads), blk(2 * heads), blk(3 * heads),
                  cblk(0), cblk(heads), tab, tab,
                  pl.BlockSpec((hd, hd), lambda hi, bi: (0, 0)), chunk],
        out_specs=[pl.BlockSpec((None, seq, hd), lambda hi, bi: (bi, 0, hi)), chunk],
        out_shape=[jax.ShapeDtypeStruct((b, seq, heads * hd), BF16), jax.ShapeDtypeStruct(cast.shape, BF16)],
        scratch_shapes=[pltpu.VMEM((seq, hd), BF16), pltpu.VMEM((seq, hd), BF16),
                        pltpu.VMEM((seq, hd), BF16), pltpu.VMEM((seq, hd), BF16),
                        pltpu.VMEM((seq // c, hd, hd), BF16),
                        pltpu.VMEM((hd, hd), F32), pltpu.VMEM((hd, hd), F32),
                        pltpu.VMEM((c, c), F32),
                        pltpu.VMEM((c, hd), F32), pltpu.VMEM((c, hd), F32),
                        pltpu.VMEM((c, hd), F32), pltpu.VMEM((c, hd), F32)],
        compiler_params=_cparams(("arbitrary", "arbitrary")),
        name="retention",
    )(log_gammas, proj3, proj3, proj3, proj3, kvc3, kvc3, cos_t, sin_t, _swap_matrix(), cast)


def _pool_kernel(u_ref, bc_ref, cnt_ref, gw_ref, ps_ref, o_ref):
    seq = u_ref.shape[0]
    pg = gw_ref.shape[1]
    blk = bc_ref.shape[1]
    n_slab = seq // GRID_W
    for gi, w in enumerate(POOL_WINDOWS):
        lo = w // 2
        hi = w - lo - 1
        ub = u_ref[:, gi * pg:(gi + 1) * pg]
        s1 = jnp.concatenate(
            [jnp.dot(bc_ref[gi], ub[r * blk:(r + 1) * blk], preferred_element_type=F32)
             for r in range(seq // blk)], axis=0)
        zpad = lambda n: jnp.zeros((n * GRID_W, pg), F32)
        a = jnp.concatenate([zpad(lo), s1, zpad(hi)], axis=0) if hi else jnp.concatenate([zpad(lo), s1], axis=0)
        step = 1
        while step < w:
            n_rows = a.shape[0] - step * GRID_W
            a = a[:n_rows] + a[step * GRID_W:]
            step *= 2
        assert a.shape[0] == n_slab * GRID_W
        cnt = cnt_ref[gi]
        m = a / (cnt if pg == 128 else jnp.concatenate([cnt] * (pg // 128), axis=1))
        d = (m - ub.astype(F32)).astype(BF16)
        y = jnp.dot(d, gw_ref[gi], preferred_element_type=F32) * ps_ref[:, gi * pg:(gi + 1) * pg]
        o_ref[:, gi * pg:(gi + 1) * pg] = y.astype(o_ref.dtype)


def _pool_tables(seq, blk):
    rows = seq // GRID_W
    bands = np.zeros((len(POOL_WINDOWS), blk, blk), np.float32)
    cnts = np.zeros((len(POOL_WINDOWS), seq, 128), np.float32)
    t = np.arange(seq)
    r, c = t // GRID_W, t % GRID_W
    for gi, w in enumerate(POOL_WINDOWS):
        lo = w // 2
        hi = w - lo - 1
        col = np.arange(GRID_W)
        band = ((col[None, :] >= col[:, None] - lo) & (col[None, :] <= col[:, None] + hi)).astype(np.float32)
        bands[gi] = np.kron(np.eye(blk // GRID_W, dtype=np.float32), band)
        cc = np.minimum(c + hi + 1, GRID_W) - np.maximum(c - lo, 0)
        cr = np.minimum(r + hi + 1, rows) - np.maximum(r - lo, 0)
        cnts[gi] = (cc * cr).astype(np.float32)[:, None]
    return jnp.asarray(bands, BF16), jnp.asarray(cnts, F32)


def _pool(proj3, col_block, pool_w, pool_scale):
    b, seq, _ = proj3.shape
    n_g, pg, _ = pool_w.shape
    pw = n_g * pg
    blk = min(256, seq)
    bands, cnts = _pool_tables(seq, blk)
    return pl.pallas_call(
        _pool_kernel,
        grid=(b,),
        in_specs=[pl.BlockSpec((None, seq, pw), lambda bi: (bi, 0, col_block)),
                  pl.BlockSpec((n_g, blk, blk), lambda bi: (0, 0, 0)),
                  pl.BlockSpec((n_g, seq, 128), lambda bi: (0, 0, 0)),
                  pl.BlockSpec((n_g, pg, pg), lambda bi: (0, 0, 0)),
                  pl.BlockSpec((1, pw), lambda bi: (0, 0))],
        out_specs=pl.BlockSpec((None, seq, pw), lambda bi: (bi, 0, 0)),
        out_shape=jax.ShapeDtypeStruct((b, seq, pw), BF16),
        compiler_params=_cparams(("parallel",)),
        name="pool",
    )(proj3, bands, cnts, pool_w, pool_scale)


def _merge_kernel(yr_ref, yp_ref, gr_ref, gp_ref, wr_ref, wp_ref, o_ref):
    a = jnp.dot(yr_ref[...], wr_ref[...], preferred_element_type=F32)
    p = jnp.dot(yp_ref[...], wp_ref[...], preferred_element_type=F32)
    o_ref[...] = (_sigmoid(gr_ref[...].astype(F32)) * a + _sigmoid(gp_ref[...].astype(F32)) * p).astype(o_ref.dtype)


def _merge(y_ret, y_pool, proj, gate_col0, w_ret_out, w_pool_out):
    t, d = y_ret.shape
    tm = min(512, t)
    tn = min(1024, d)
    nj = d // tn
    return pl.pallas_call(
        _merge_kernel,
        grid=(t // tm, nj),
        in_specs=[pl.BlockSpec((tm, d), lambda i, j: (i, 0)),
                  pl.BlockSpec((tm, d), lambda i, j: (i, 0)),
                  pl.BlockSpec((tm, tn), lambda i, j: (i, gate_col0 * nj + j)),
                  pl.BlockSpec((tm, tn), lambda i, j: (i, (gate_col0 + 1) * nj + j)),
                  pl.BlockSpec((d, tn), lambda i, j: (0, j)),
                  pl.BlockSpec((d, tn), lambda i, j: (0, j))],
        out_specs=pl.BlockSpec((tm, tn), lambda i, j: (i, j)),
        out_shape=jax.ShapeDtypeStruct((t, d), BF16),
        compiler_params=_cparams(("parallel", "parallel")),
        name="merge",
    )(y_ret, y_pool, proj, proj, w_ret_out, w_pool_out)


NEG_BIG = -3.0e38
LANE_NONE = 1.0e9


def _route_block(logits):
    lane = lax.broadcasted_iota(jnp.int32, logits.shape, 1).astype(F32)

    def first_max(v):
        m = jnp.max(v, axis=1, keepdims=True)
        return m, jnp.min(jnp.where(v == m, lane, LANE_NONE), axis=1, keepdims=True)

    is_g = lane < N_GROUPS
    gmax, g_idx = first_max(jnp.where(is_g, logits, NEG_BIG))
    gden = jnp.sum(jnp.where(is_g, jnp.exp(logits - gmax), 0.0), axis=1, keepdims=True)
    lo = N_GROUPS + EXPERTS_PER_GROUP * g_idx
    in_grp = (lane >= lo) & (lane < lo + EXPERTS_PER_GROUP)
    le = jnp.where(in_grp, logits, NEG_BIG)
    m1, i1 = first_max(le)
    m2, i2 = first_max(jnp.where(lane == i1, NEG_BIG, le))
    eden = jnp.sum(jnp.where(in_grp, jnp.exp(logits - m1), 0.0), axis=1, keepdims=True)
    pg = 1.0 / gden
    w1 = pg * (1.0 / eden)
    w2 = pg * (jnp.exp(m2 - m1) / eden)
    return w1, w2, i1 - N_GROUPS, i2 - N_GROUPS


def _post_kernel(m_ref, x_ref, wo_ref, gpm_ref, gpf_ref, gt_ref, sh_ref, sc_ref, wr_ref, br_ref, tri_ref,
                 x1_ref, h2_ref, rt_ref, cnt_ref, carry):
    @pl.when(pl.program_id(0) == 0)
    def _():
        carry[...] = jnp.zeros(carry.shape, F32)

    tm, d = x_ref.shape
    sub = tri_ref.shape[0]
    nl = d // LANE_TILE
    counts = carry[...]
    for sb in range(tm // sub):
        r = slice(sb * sub, (sb + 1) * sub)
        mix = jnp.dot(m_ref[r, :], wo_ref[...], preferred_element_type=F32)
        x1 = x_ref[r, :] + gt_ref[0] * _rms(mix, gpm_ref[...])
        x1_ref[r, :] = x1
        h2 = _rms(x1, gpf_ref[...]) * (1.0 + sc_ref[0]) + sh_ref[0]
        for j in range(nl):
            h2_ref[pl.ds(sb * sub * nl + j, sub, stride=nl), :] = h2[:, j * LANE_TILE:(j + 1) * LANE_TILE]
        logits = jnp.dot(h2.astype(BF16), wr_ref[...], preferred_element_type=F32) + br_ref[...]
        w1, w2, e1, e2 = _route_block(logits)

        lane = lax.broadcasted_iota(jnp.int32, logits.shape, 1).astype(F32)
        oh1 = lane == e1
        oh2 = lane == e2
        both = jnp.where(oh1 | oh2, 1.0, 0.0)
        base = jnp.dot(tri_ref[...], both.astype(BF16), preferred_element_type=F32) + counts[0:1, :]
        r1 = jnp.sum(jnp.where(oh1, base, 0.0), axis=1, keepdims=True)
        r2 = jnp.sum(jnp.where(oh2, base, 0.0), axis=1, keepdims=True)
        counts = counts + jnp.sum(both, axis=0, keepdims=True)

        out = jnp.zeros(logits.shape, F32)
        for k, col in enumerate((w1, w2, e1, e2, r1, r2)):
            out = jnp.where(lane == float(k), col, out)
        rt_ref[r, :] = out
    carry[...] = counts
    cnt_ref[...] = counts


def _post(merged, x2, w_o, g_post_mix, g_pre_ffn, gt_m, sh_f, sc_f, w_router, b_router, seq):
    t, d = x2.shape
    tm = min(512, seq)
    sub = min(256, tm)
    per_b = seq // tm
    nl = d // LANE_TILE
    vec = pl.BlockSpec((1, d), lambda i: (0, 0))
    bvec = pl.BlockSpec((1, 1, d), lambda i: (i // per_b, 0, 0))
    row = pl.BlockSpec((tm, d), lambda i: (i, 0))
    tri = jnp.asarray(np.tril(np.ones((sub, sub), np.float32), -1), BF16)
    return pl.pallas_call(
        _post_kernel,
        grid=(t // tm,),
        in_specs=[row, row,
                  pl.BlockSpec((d, d), lambda i: (0, 0), pipeline_mode=pl.Buffered(1)),
                  vec, vec, bvec, bvec, bvec,
                  pl.BlockSpec((d, ROUTER_LANES), lambda i: (0, 0)),
                  pl.BlockSpec((1, ROUTER_LANES), lambda i: (0, 0)),
                  pl.BlockSpec((sub, sub), lambda i: (0, 0))],
        out_specs=[row, pl.BlockSpec((tm * nl, LANE_TILE), lambda i: (i, 0)),
                   pl.BlockSpec((tm, ROUTER_LANES), lambda i: (i, 0)),
                   pl.BlockSpec((8, ROUTER_LANES), lambda i: (0, 0))],
        out_shape=[jax.ShapeDtypeStruct((t, d), F32), jax.ShapeDtypeStruct((t * nl, LANE_TILE), F32),
                   jax.ShapeDtypeStruct((t, ROUTER_LANES), F32),
                   jax.ShapeDtypeStruct((8, ROUTER_LANES), F32)],
        scratch_shapes=[pltpu.VMEM((8, ROUTER_LANES), F32)],
        compiler_params=_cparams(("arbitrary",)),
        name="post",
    )(merged, x2, w_o, g_post_mix, g_pre_ffn, gt_m, sh_f, sc_f, w_router, b_router, tri)


def _expert_kernel(be_ref, nu_ref, rt_ref, cb_ref, ce_ref, first_ref, par_ref, nxt_ref, h2_ref, wg_ref, wu_ref,
                   wd_ref, o_ref, xbuf, wgb, wub, wdb, sems, wsems):
    i = pl.program_id(0)
    n = pl.num_programs(0)
    nu = nu_ref[0]
    ns = xbuf.shape[0]
    nl = wgb.shape[1] // LANE_TILE
    bm = xbuf.shape[1] // nl
    n_experts = wg_ref.shape[0]

    def weight_copies(e, s):
        return [pltpu.make_async_copy(src.at[e], dst.at[s], wsems.at[s])
                for src, dst in ((wg_ref, wgb), (wu_ref, wub), (wd_ref, wdb))]

    @pl.when(i == 0)
    def _():
        for cp in weight_copies(be_ref[0], 0):
            cp.start()

    @pl.when((i < nu) & (first_ref[i] == 1))
    def _():
        wslot = par_ref[i]
        for cp in weight_copies(be_ref[i], wslot):
            cp.wait()

        @pl.when(nxt_ref[i] < n_experts)
        def _():
            for cp in weight_copies(nxt_ref[i], 1 - wslot):
                cp.start()

    def row_copy(step, r, s):
        tok = rt_ref[jnp.minimum(cb_ref[step] + r, ce_ref[step])]
        src = pl.multiple_of(tok * nl, nl)
        dst = r * nl if isinstance(r, int) else pl.multiple_of(r * nl, nl)
        return pltpu.make_async_copy(h2_ref.at[pl.ds(src, nl)], xbuf.at[s, pl.ds(dst, nl)], sems.at[s])

    def wait_slot(s):
        pltpu.make_async_copy(h2_ref.at[pl.ds(0, bm * nl)], xbuf.at[s], sems.at[s]).wait()

    @pl.when(i == 0)
    def _():
        for blk in range(ns - 1):
            def body(r, carry):
                row_copy(blk, r, blk).start()
                return carry
            lax.fori_loop(0, bm, body, 0)

    @pl.when(i < nu)
    def _():
        slot = i % ns
        wait_slot(slot)
        view = xbuf.at[slot]
        x = jnp.concatenate([view[pl.ds(j, bm, stride=nl), :] for j in range(nl)], axis=1).astype(BF16)
        ahead = jnp.minimum(i + ns - 1, n - 1)
        s_ahead = (i + ns - 1) % ns
        for r in range(bm):
            row_copy(ahead, r, s_ahead).start()
        wslot = par_ref[i]
        g = jnp.dot(x, wgb[wslot], preferred_element_type=F32)
        u = jnp.dot(x, wub[wslot], preferred_element_type=F32)
        a = (_silu(g) * u).astype(BF16)
        y = jnp.dot(a, wdb[wslot], preferred_element_type=F32)
        for j in range(nl):
            o_ref[pl.ds(j, bm, stride=nl), :] = y[:, j * LANE_TILE:(j + 1) * LANE_TILE]

    @pl.when(i >= nu)
    def _():
        o_ref[...] = jnp.zeros(o_ref.shape, o_ref.dtype)

    @pl.when((i == nu) | ((i == n - 1) & (i < nu)))
    def _():
        last = jnp.where(i < nu, i + 1, i)
        for k in range(ns - 1):
            wait_slot((last + k) % ns)


def _experts(h2, sorted_tok, blk_base, blk_last, block_e, n_used, run_first, run_parity, next_e, w_gate, w_up, w_down):
    n_blocks = block_e.shape[0]
    bm = MOE_ROWS
    assert n_blocks >= EXPERT_SLOTS
    d, ff = w_gate.shape[1:]
    nl = d // LANE_TILE
    hbm = pl.BlockSpec(memory_space=pl.ANY)
    return pl.pallas_call(
        _expert_kernel,
        grid_spec=pltpu.PrefetchScalarGridSpec(
            num_scalar_prefetch=8,
            grid=(n_blocks,),
            in_specs=[hbm, hbm, hbm, hbm],
            out_specs=pl.BlockSpec((bm * nl, LANE_TILE), lambda i, *_: (i, 0)),
            scratch_shapes=[pltpu.VMEM((EXPERT_SLOTS, bm * nl, LANE_TILE), F32),
                            pltpu.VMEM((2, d, ff), BF16), pltpu.VMEM((2, d, ff), BF16),
                            pltpu.VMEM((2, ff, d), BF16),
                            pltpu.SemaphoreType.DMA((EXPERT_SLOTS,)), pltpu.SemaphoreType.DMA((2,))]),
        out_shape=jax.ShapeDtypeStruct((n_blocks * bm * nl, LANE_TILE), F32),
        compiler_params=_cparams(("arbitrary",)),
        name="experts",
    )(block_e, n_used, sorted_tok, blk_base, blk_last, run_first, run_parity, next_e, h2, w_gate, w_up, w_down)


def _combine_kernel(pos_ref, wt_ref, x1_ref, g_ref, gt_ref, yb_ref, o_ref, buf, sems):
    i = pl.program_id(0)
    n = pl.num_programs(0)
    tm, d = x1_ref.shape
    nl = d // LANE_TILE
    ns = buf.shape[0]
    slot = i % ns

    def row_copy(step, r, kk, s):
        row = pl.multiple_of(pos_ref[step, TOP_K * r + kk] * nl, nl)
        dst = r * nl if isinstance(r, int) else pl.multiple_of(r * nl, nl)
        return pltpu.make_async_copy(yb_ref.at[pl.ds(row, nl)], buf.at[s, kk, pl.ds(dst, nl)], sems.at[s])

    def wait_slot(s):
        for kk in range(TOP_K):
            pltpu.make_async_copy(yb_ref.at[pl.ds(0, tm * nl)], buf.at[s, kk], sems.at[s]).wait()

    @pl.when(i == 0)
    def _():
        for blk in range(ns - 1):
            def body(r, carry):
                for kk in range(TOP_K):
                    row_copy(jnp.minimum(blk, n - 1), r, kk, blk).start()
                return carry
            lax.fori_loop(0, tm, body, 0)

    wait_slot(slot)
    ahead = jnp.minimum(i + ns - 1, n - 1)
    s_ahead = (i + ns - 1) % ns
    for r in range(tm):
        for kk in range(TOP_K):
            row_copy(ahead, r, kk, s_ahead).start()

    def rows(kk):
        view = buf.at[slot, kk]
        return jnp.concatenate([view[pl.ds(j, tm, stride=nl), :] for j in range(nl)], axis=1)

    wt = wt_ref[...]
    ffn = rows(0) * wt[:, 0:1] + rows(1) * wt[:, 1:2]
    o_ref[...] = x1_ref[...] + gt_ref[0] * _rms(ffn, g_ref[...])

    @pl.when(i == n - 1)
    def _():
        for k in range(1, ns):
            wait_slot((i + k) % ns)


def _combine(yb, pos2, wts, x1, g_post_ffn, gt_f, seq):
    t, d = x1.shape
    tm = pos2.shape[1] // TOP_K
    per_b = seq // tm
    return pl.pallas_call(
        _combine_kernel,
        grid_spec=pltpu.PrefetchScalarGridSpec(
            num_scalar_prefetch=1,
            grid=(t // tm,),
            in_specs=[pl.BlockSpec((tm, TOP_K), lambda i, p: (i, 0)),
                      pl.BlockSpec((tm, d), lambda i, p: (i, 0)),
                      pl.BlockSpec((1, d), lambda i, p: (0, 0)),
                      pl.BlockSpec((1, 1, d), lambda i, p: (i // per_b, 0, 0)),
                      pl.BlockSpec(memory_space=pl.ANY)],
            out_specs=pl.BlockSpec((tm, d), lambda i, p: (i, 0)),
            scratch_shapes=[pltpu.VMEM((COMBINE_SLOTS, TOP_K, tm * (d // LANE_TILE), LANE_TILE), F32),
                            pltpu.SemaphoreType.DMA((COMBINE_SLOTS,))]),
        out_shape=jax.ShapeDtypeStruct((t, d), F32),
        compiler_params=_cparams(("arbitrary",)),
        name="combine",
    )(pos2, wts, x1, g_post_ffn, gt_f, yb)


def _rope_tables(seq):
    quarter = HEAD_DIM // 4
    freqs = ROPE_BASE ** (-jnp.arange(quarter, dtype=F32) / quarter)
    pos = jnp.arange(seq)
    rows = (pos // GRID_W).astype(F32)
    cols = (pos % GRID_W).astype(F32)
    ar = rows[:, None] * freqs[None, :]
    ac = cols[:, None] * freqs[None, :]
    cos_t = jnp.concatenate([jnp.cos(ar), jnp.cos(ar), jnp.cos(ac), jnp.cos(ac)], axis=1)
    sin_t = jnp.concatenate([-jnp.sin(ar), jnp.sin(ar), -jnp.sin(ac), jnp.sin(ac)], axis=1)
    return cos_t, sin_t


def _dispatch(route, counts, t):
    bm = MOE_ROWS
    n_assign = t * TOP_K
    e = route[:, 2:4].astype(jnp.int32)
    rank = route[:, 4:6].astype(jnp.int32)
    counts = counts.astype(jnp.int32)
    padded = ((counts + bm - 1) // bm) * bm
    pend = jnp.cumsum(padded)
    pstart = pend - padded
    onehot = e[:, :, None] == jnp.arange(N_EXPERTS, dtype=jnp.int32)[None, None, :]
    pos = (jnp.sum(jnp.where(onehot, pstart[None, None, :], 0), axis=-1) + rank).reshape(-1)
    n_blocks = -(-n_assign // bm) + N_EXPERTS
    sorted_tok = (jnp.argsort(pos) // TOP_K).astype(jnp.int32)
    block_start = jnp.arange(n_blocks, dtype=jnp.int32) * bm
    block_e = jnp.minimum(jnp.sum(pend[None, :] <= block_start[:, None], axis=1), N_EXPERTS - 1).astype(jnp.int32)
    n_used = (pend[-1] // bm).astype(jnp.int32).reshape(1)
    active = jnp.arange(n_blocks, dtype=jnp.int32) < n_used[0]
    of_block = lambda table: jnp.sum(jnp.where(block_e[:, None] == jnp.arange(N_EXPERTS)[None, :],
                                               table[None, :], 0), axis=1)
    start = jnp.cumsum(counts) - counts
    blk_base = jnp.where(active, of_block(start - pstart) + block_start, 0).astype(jnp.int32)
    blk_last = jnp.where(active, of_block(start + counts - 1), 0).astype(jnp.int32)
    prev_e = jnp.concatenate([jnp.full((1,), -1, jnp.int32), block_e[:-1]])
    run_first = (active & (block_e != prev_e)).astype(jnp.int32)
    run_parity = jnp.maximum(jnp.cumsum(run_first) - 1, 0).astype(jnp.int32) % 2
    ids = jnp.arange(N_EXPERTS, dtype=jnp.int32)
    live = jnp.where(counts > 0, ids, N_EXPERTS)
    live_from = lax.cummin(live, reverse=True)
    next_live = jnp.concatenate([live_from[1:], jnp.full((1,), N_EXPERTS, jnp.int32)])
    next_e = jnp.sum(jnp.where(block_e[:, None] == ids[None, :], next_live[None, :], 0), axis=1).astype(jnp.int32)
    return sorted_tok, blk_base, blk_last, block_e, pos, n_used, run_first, run_parity, next_e


def kernel(x, c, ctx, c_ctx, w_mod, b_mod, g_pre_mix, g_post_mix, g_pre_ffn, g_post_ffn, w_in, ret_decay, pool_w, pool_scale, w_ret_out, w_pool_out, w_o, w_router_group, b_router_group, w_router_expert, b_router_expert, w_exp_gate, w_exp_up, w_exp_down):
    b, seq, d = x.shape
    lc = ctx.shape[1]
    heads = d // HEAD_DIM
    t = b * seq
    assert w_mod.shape[0] == 1, "single layer"
    x2 = x.reshape(t, d)

    pad = (-(b + 1)) % 8
    cc = jnp.concatenate([c, c_ctx[None, :], jnp.zeros((pad, d), F32)], axis=0)
    mod = _mod(cc, w_mod[0], b_mod[0][None, :])
    sh_m, sc_m, gt_m, sh_f, sc_f, gt_f = [mod[:b, k * d:(k + 1) * d].reshape(b, 1, d) for k in range(6)]
    csh_m = mod[b:b + 1, 0:d].reshape(1, 1, d)
    csc_m = mod[b:b + 1, d:2 * d].reshape(1, 1, d)
    log_gammas = -jnp.exp(ret_decay[0].astype(F32))

    w_in_b = w_in[0].astype(BF16)
    tn = min(1024, d)
    per_seg = d // tn
    n_e, _, ff = w_exp_gate.shape[1:]
    proj, (wg_b, wu_b) = _inproj(x2, g_pre_mix[0][None, :], sh_m, sc_m, w_in_b, seq, 0, 7 * per_seg, tn,
                                 cast=(w_exp_gate[0].reshape(n_e * d, ff), w_exp_up[0].reshape(n_e * d, ff)))
    kvc, _ = _inproj(ctx.reshape(b * lc, d), g_pre_mix[0][None, :], csh_m, csc_m, w_in_b, b * lc,
                     per_seg, 2 * per_seg, tn)
    proj3 = proj.reshape(b, seq, 7 * d)
    kvc3 = kvc.reshape(b, lc, 2 * d)

    cos_t, sin_t = _rope_tables(seq)
    y_ret, wd_b = _retention(proj3, kvc3, log_gammas, cos_t, sin_t, heads, w_exp_down[0].reshape(n_e * ff, d))
    y_ret = y_ret.reshape(t, d)
    y_pool = _pool(proj3, 4, pool_w[0].astype(BF16), pool_scale[0][None, :]).reshape(t, d)

    merged = _merge(y_ret, y_pool, proj, 5, w_ret_out[0].astype(BF16), w_pool_out[0].astype(BF16))

    n_r = N_GROUPS + N_EXPERTS
    w_router = jnp.concatenate([w_router_group[0], w_router_expert[0],
                                jnp.zeros((d, ROUTER_LANES - n_r), F32)], axis=1).astype(BF16)
    b_router = jnp.concatenate([b_router_group[0], b_router_expert[0],
                                jnp.zeros((ROUTER_LANES - n_r,), F32)])[None, :]
    x1, h2, route, counts = _post(merged, x2, w_o[0].astype(BF16), g_post_mix[0][None, :],
                                  g_pre_ffn[0][None, :], gt_m, sh_f, sc_f, w_router, b_router, seq)

    (sorted_tok, blk_base, blk_last, block_e, pos, n_used, run_first, run_parity,
     next_e) = _dispatch(route, counts[0, :N_EXPERTS], t)
    yb = _experts(h2, sorted_tok, blk_base, blk_last, block_e, n_used, run_first, run_parity, next_e,
                  wg_b.reshape(n_e, d, ff), wu_b.reshape(n_e, d, ff), wd_b.reshape(n_e, ff, d))
    tm_c = min(256, seq)
    out = _combine(yb, pos.reshape(-1, TOP_K * tm_c), route[:, 0:TOP_K], x1, g_post_ffn[0][None, :], gt_f, seq)
    return out.reshape(b, seq, d)
```
